```python
import jax, jax.numpy as jnp
from jax import lax
import numpy as np

D_MODEL = 2048
BATCH = 32
SEQ = 256
DEPTH = 2
DEC_BATCH = 8
DEC_SEQ = 2048
PAST_LEN = 256

GRID_W = 64
MIX_W = D_MODEL
A_W = MIX_W // 4
A_GROUPS = 4
A_GD = A_W // A_GROUPS
CHUNK = 128
B_W = MIX_W // 4
B_HEADS = 8
B_HD = B_W // B_HEADS
CONV_W = 4
RG_C = 8.0
C_W = MIX_W // 2
HEAD_DIM = 64
N_HEADS = C_W // HEAD_DIM
N_KV = 4
GQA_G = N_HEADS // N_KV
KV_W = N_KV * HEAD_DIM
WINDOW = 128
WBLK = 128
ROPE_BASE = 10000.0
ATTN_SCALE = HEAD_DIM ** -0.5
N_EXP = 32
TOP_K = 4
D_FF = D_MODEL
SWIGLU_LIMIT = 7.0
SWIGLU_ALPHA = 1.702
MOE_BLK = 128
EPS = 1e-6
IN_COLS = 2 * A_W + 2 * B_W + C_W + 2 * KV_W
SPLITS = (A_W, 2 * A_W, 2 * A_W + B_W, 2 * A_W + 2 * B_W, 2 * A_W + 2 * B_W + C_W, 2 * A_W + 2 * B_W + C_W + KV_W)

kernel_name = "hybrid_dit_prefix_ctx_step"

F32 = jnp.float32


def rmsnorm(x, g):
    xf = x.astype(F32)
    y = xf * lax.rsqrt(jnp.mean(xf * xf, axis=-1, keepdims=True) + EPS)
    return (y * g.astype(F32)).astype(x.dtype)


def modulate(h, shift, scale):
    return h * (1 + scale) + shift


def chunk_mlp(u, v, w_sp, b_sp):
    b, n, _ = v.shape
    vc = v.reshape(b, n // CHUNK, CHUNK, A_GROUPS, A_GD)
    s = jnp.einsum('gpq,bcqgd->bcpgd', w_sp, vc) + b_sp.T[:, :, None]
    return u * s.reshape(b, n, A_W)


def _lin_combine(left, right):
    a_l, b_l = left
    a_r, b_r = right
    return a_l * a_r, a_r * b_l + b_r


def rglru_direction(x, conv_w, conv_b, w_r, b_r, w_i, b_i, lam, h0):
    x = lax.conv_general_dilated(x, conv_w[:, None, :], window_strides=(1,), padding=[(CONV_W - 1, 0)],
                                 dimension_numbers=('NWC', 'WIO', 'NWC'), feature_group_count=B_W) + conv_b
    b, n, _ = x.shape
    xh = x.reshape(b, n, B_HEADS, B_HD)
    r = jax.nn.sigmoid((jnp.einsum('blhi,hij->blhj', xh, w_r).reshape(b, n, B_W) + b_r).astype(F32))
    i = jax.nn.sigmoid((jnp.einsum('blhi,hij->blhj', xh, w_i).reshape(b, n, B_W) + b_i).astype(F32))
    log_a = -RG_C * r * jax.nn.softplus(-lam.astype(F32))
    a = jnp.exp(log_a)
    bx = jnp.sqrt(-jnp.expm1(2.0 * log_a)) * (i * x.astype(F32))
    bx = bx.at[:, 0].add(a[:, 0] * h0)
    _, h = lax.associative_scan(_lin_combine, (a, bx), axis=1)
    return h, h[:, -1]


def recurrent_mixer(xr, gr, conv_w, conv_b, w_r, b_r, w_i, b_i, lam, h0):
    h_f, last_f = rglru_direction(xr, conv_w[0], conv_b[0], w_r[0], b_r[0], w_i[0], b_i[0], lam[0], h0[:, 0])
    h_b, last_b = rglru_direction(xr[:, ::-1], conv_w[1], conv_b[1], w_r[1], b_r[1], w_i[1], b_i[1], lam[1], h0[:, 1])
    y = jax.nn.gelu(gr.astype(F32)) * (h_f + h_b[:, ::-1])
    return y.astype(xr.dtype), jnp.stack([last_f, last_b], axis=1)


def axial_rope(n):
    rows = n // GRID_W
    row = jnp.repeat(jnp.arange(rows), GRID_W).astype(F32)
    col = jnp.tile(jnp.arange(GRID_W), rows).astype(F32)
    n_freq = HEAD_DIM // 4
    inv = ROPE_BASE ** (-jnp.arange(n_freq, dtype=F32) / n_freq)
    ang = jnp.stack([row[:, None] * inv, col[:, None] * inv], axis=1)
    return jnp.cos(ang), jnp.sin(ang)


def apply_rope(x, cos, sin):
    xf = x.astype(F32).reshape(*x.shape[:-1], 2, 2, HEAD_DIM // 4)
    x1, x2 = xf[..., 0, :], xf[..., 1, :]
    out = jnp.stack([x1 * cos - x2 * sin, x2 * cos + x1 * sin], axis=-2)
    return out.reshape(x.shape).astype(x.dtype)


def joint_softmax(pieces, sink):
    m = sink
    for s in pieces:
        m = jnp.maximum(m, s.max(axis=-1, keepdims=True))
    exps = [jnp.exp(s - m) for s in pieces]
    denom = jnp.exp(sink - m)
    for e in exps:
        denom = denom + e.sum(axis=-1, keepdims=True)
    return [e / denom for e in exps]


def context_attention(q, k, v, sinks):
    s = jnp.einsum('bkgqd,bksd->bkgqs', q, k, preferred_element_type=F32) * ATTN_SCALE
    (p,) = joint_softmax([s], sinks.astype(F32).reshape(1, N_KV, GQA_G, 1, 1))
    return jnp.einsum('bkgqs,bksd->bkgqd', p.astype(v.dtype), v)


def latent_attention(q, k, v, kc, vc, sinks):
    b, _, _, n, _ = q.shape
    nb = n // WBLK
    qb = q.reshape(b, N_KV, GQA_G, nb, WBLK, HEAD_DIM)

    def windows(t):
        tp = jnp.pad(t, ((0, 0), (0, 0), (WBLK, WBLK), (0, 0))).reshape(b, N_KV, nb + 2, WBLK, HEAD_DIM)
        return jnp.concatenate([tp[:, :, :-2], tp[:, :, 1:-1], tp[:, :, 2:]], axis=3)

    kw, vw = windows(k), windows(v)
    qpos = jnp.arange(nb)[:, None, None] * WBLK + jnp.arange(WBLK)[None, :, None]
    kpos = (jnp.arange(nb)[:, None, None] - 1) * WBLK + jnp.arange(3 * WBLK)[None, None, :]
    mask = (jnp.abs(kpos - qpos) <= WINDOW) & (kpos >= 0) & (kpos < n)
    s_loc = jnp.einsum('bkgnqd,bknsd->bkgnqs', qb, kw, preferred_element_type=F32) * ATTN_SCALE
    s_loc = jnp.where(mask, s_loc, -jnp.inf)
    s_ctx = jnp.einsum('bkgnqd,bksd->bkgnqs', qb, kc, preferred_element_type=F32) * ATTN_SCALE
    p_loc, p_ctx = joint_softmax([s_loc, s_ctx], sinks.astype(F32).reshape(1, N_KV, GQA_G, 1, 1, 1))
    o = (jnp.einsum('bkgnqs,bknsd->bkgnqd', p_loc.astype(v.dtype), vw)
         + jnp.einsum('bkgnqs,bksd->bkgnqd', p_ctx.astype(vc.dtype), vc))
    return o.reshape(b, N_KV, GQA_G, n, HEAD_DIM)


def moe_ffn(h, w_router, b_router, w_gu, b_gu, w_down, b_down):
    shp = h.shape
    x = h.reshape(-1, D_MODEL)
    t = x.shape[0]
    logits = (x @ w_router + b_router).astype(F32)
    top_v, top_e = lax.top_k(logits, TOP_K)
    gates = jax.nn.softmax(top_v, axis=-1)
    flat_e = top_e.reshape(-1)
    flat_tok = jnp.repeat(jnp.arange(t, dtype=jnp.int32), TOP_K)
    order = jnp.argsort(flat_e)
    se, stok, sg = flat_e[order], flat_tok[order], gates.reshape(-1)[order]
    counts = jnp.bincount(flat_e, length=N_EXP)
    padded = (counts + MOE_BLK - 1) // MOE_BLK * MOE_BLK
    pad_end = jnp.cumsum(padded)
    pad_start = pad_end - padded
    start = jnp.cumsum(counts) - counts
    dest = pad_start[se] + jnp.arange(t * TOP_K) - start[se]
    n_blocks = -(-(t * TOP_K) // MOE_BLK) + N_EXP
    slot_tok = jnp.full((n_blocks * MOE_BLK,), t, jnp.int32).at[dest].set(stok)
    blk_exp = jnp.minimum(jnp.sum(pad_end[None, :] <= jnp.arange(n_blocks)[:, None] * MOE_BLK, axis=1), N_EXP - 1)
    x_pad = jnp.concatenate([x, jnp.zeros((1, D_MODEL), x.dtype)], axis=0)

    def expert_block(args):
        tok, e = args
        gu = x_pad[tok] @ w_gu[e] + b_gu[e]
        g = jnp.minimum(gu[:, :D_FF], SWIGLU_LIMIT)
        u = jnp.clip(gu[:, D_FF:], -SWIGLU_LIMIT, SWIGLU_LIMIT)
        act = (u + 1) * (g * jax.nn.sigmoid(SWIGLU_ALPHA * g))
        return act @ w_down[e] + b_down[e]

    y_blk = lax.map(expert_block, (slot_tok.reshape(n_blocks, MOE_BLK), blk_exp))
    y = y_blk.reshape(-1, D_MODEL)[dest] * sg[:, None].astype(x.dtype)
    return jax.ops.segment_sum(y, stok, num_segments=t).reshape(shp)


def layer_forward(x, mod, p, rope=None, ctx=None):
    shift1, scale1, gate1, shift2, scale2, gate2 = jnp.split(mod, 6, axis=-1)
    b, n, _ = x.shape
    h = modulate(rmsnorm(x, p['g_mix']), shift1, scale1)
    u, v, xr, gr, q, k, vv = jnp.split(h @ p['w_in'], SPLITS, axis=-1)
    a_out = chunk_mlp(jax.nn.gelu(u), jax.nn.gelu(v), p['w_sp'], p['b_sp'])
    h0 = jnp.zeros((b, 2, B_W), F32) if ctx is None else ctx[2]
    b_out, lru_last = recurrent_mixer(xr, gr, p['conv_w'], p['conv_b'], p['w_rg'], p['b_rg'],
                                      p['w_ig'], p['b_ig'], p['lam'], h0)
    q = q.reshape(b, n, N_KV, GQA_G, HEAD_DIM).transpose(0, 2, 3, 1, 4)
    k = k.reshape(b, n, N_KV, HEAD_DIM).transpose(0, 2, 1, 3)
    vv = vv.reshape(b, n, N_KV, HEAD_DIM).transpose(0, 2, 1, 3)
    if ctx is None:
        c_out = context_attention(q, k, vv, p['sinks'])
    else:
        cos, sin = rope
        c_out = latent_attention(apply_rope(q, cos, sin), apply_rope(k, cos, sin), vv, ctx[0], ctx[1], p['sinks'])
    c_out = c_out.transpose(0, 3, 1, 2, 4).reshape(b, n, C_W)
    mix = jnp.concatenate([a_out, b_out, c_out], axis=-1) @ p['w_out']
    x = x + gate1 * mix
    h2 = modulate(rmsnorm(x, p['g_ffn']), shift2, scale2)
    x = x + gate2 * moe_ffn(h2, p['w_router'], p['b_router'], p['w_gu'], p['b_gu'], p['w_down'], p['b_down'])
    return x, k, vv, lru_last


def setup_inputs(seed: int = 0) -> dict:
    key = jax.random.key(seed)
    keys = iter(jax.random.split(key, 40))

    def nrm(shape, scale):
        return scale * jax.random.normal(next(keys), shape, F32)

    x_prompt = nrm((BATCH, SEQ, D_MODEL), 1.0)
    x_sample = nrm((DEC_BATCH, DEC_SEQ, D_MODEL), 1.0)
    cache_k = nrm((DEC_BATCH, DEPTH, N_KV, PAST_LEN, HEAD_DIM), 1.0)
    cache_v = nrm((DEC_BATCH, DEPTH, N_KV, PAST_LEN, HEAD_DIM), 1.0)
    state_lru = nrm((DEC_BATCH, DEPTH, 2, B_W), 0.5)
    c = nrm((DEC_BATCH, D_MODEL), 1.0)
    c_ctx = nrm((D_MODEL,), 1.0)
    w_mod = nrm((DEPTH, D_MODEL, 6 * D_MODEL), 0.5 * D_MODEL ** -0.5)
    b_mod = nrm((DEPTH, 6 * D_MODEL), 0.02)
    g_mix = 1.0 + nrm((DEPTH, D_MODEL), 0.05)
    w_in = nrm((DEPTH, D_MODEL, IN_COLS), D_MODEL ** -0.5)
    w_sp = nrm((DEPTH, A_GROUPS, CHUNK, CHUNK), CHUNK ** -0.5)
    b_sp = 1.0 + nrm((DEPTH, A_GROUPS, CHUNK), 0.1)
    conv_w = nrm((DEPTH, 2, CONV_W, B_W), CONV_W ** -0.5)
    conv_b = nrm((DEPTH, 2, B_W), 0.02)
    w_rg = nrm((DEPTH, 2, B_HEADS, B_HD, B_HD), B_HD ** -0.5)
    b_rg = nrm((DEPTH, 2, B_W), 0.02)
    w_ig = nrm((DEPTH, 2, B_HEADS, B_HD, B_HD), B_HD ** -0.5)
    b_ig = nrm((DEPTH, 2, B_W), 0.02)
    a_pow = jax.random.uniform(next(keys), (DEPTH, 2, B_W), F32, 0.9, 0.999)
    s_lam = a_pow ** (1.0 / RG_C)
    lru_lambda = jnp.log(s_lam) - jnp.log1p(-s_lam)
    sinks = nrm((DEPTH, N_HEADS), 0.5)
    w_out = nrm((DEPTH, MIX_W, D_MODEL), MIX_W ** -0.5)
    g_ffn = 1.0 + nrm((DEPTH, D_MODEL), 0.05)
    w_router = nrm((DEPTH, D_MODEL, N_EXP), D_MODEL ** -0.5)
    b_router = nrm((DEPTH, N_EXP), 0.01)
    w_gu = nrm((DEPTH, N_EXP, D_MODEL, 2 * D_FF), D_MODEL ** -0.5)
    b_gu = nrm((DEPTH, N_EXP, 2 * D_FF), 0.02)
    w_down = nrm((DEPTH, N_EXP, D_FF, D_MODEL), D_FF ** -0.5)
    b_down = nrm((DEPTH, N_EXP, D_MODEL), 0.02)
    g_final = 1.0 + nrm((D_MODEL,), 0.05)
    return {"x_prompt": x_prompt, "x_sample": x_sample, "cache_k": cache_k, "cache_v": cache_v,
            "state_lru": state_lru, "c": c, "c_ctx": c_ctx, "w_mod": w_mod, "b_mod": b_mod,
            "g_mix": g_mix, "w_in": w_in, "w_sp": w_sp, "b_sp": b_sp, "conv_w": conv_w, "conv_b": conv_b,
            "w_rg": w_rg, "b_rg": b_rg, "w_ig": w_ig, "b_ig": b_ig, "lru_lambda": lru_lambda,
            "sinks": sinks, "w_out": w_out, "g_ffn": g_ffn, "w_router": w_router, "b_router": b_router,
            "w_gu": w_gu, "b_gu": b_gu, "w_down": w_down, "b_down": b_down, "g_final": g_final}


def reference(x_prompt, x_sample, cache_k, cache_v, state_lru, c, c_ctx, w_mod, b_mod, g_mix, w_in,
              w_sp, b_sp, conv_w, conv_b, w_rg, b_rg, w_ig, b_ig, lru_lambda, sinks, w_out, g_ffn,
              w_router, b_router, w_gu, b_gu, w_down, b_down, g_final):
    rope = axial_rope(x_sample.shape[1])
    xp, xs = x_prompt, x_sample
    ks, vs, ss = [], [], []
    for l in range(DEPTH):
        p = {'g_mix': g_mix[l], 'w_in': w_in[l], 'w_sp': w_sp[l], 'b_sp': b_sp[l], 'conv_w': conv_w[l],
             'conv_b': conv_b[l], 'w_rg': w_rg[l], 'b_rg': b_rg[l], 'w_ig': w_ig[l], 'b_ig': b_ig[l],
             'lam': lru_lambda[l], 'sinks': sinks[l], 'w_out': w_out[l], 'g_ffn': g_ffn[l],
             'w_router': w_router[l], 'b_router': b_router[l], 'w_gu': w_gu[l], 'b_gu': b_gu[l],
             'w_down': w_down[l], 'b_down': b_down[l]}
        mod_ctx = jax.nn.silu(c_ctx) @ w_mod[l] + b_mod[l]
        mod_lat = (jax.nn.silu(c) @ w_mod[l] + b_mod[l])[:, None, :]
        xp, k_ctx, v_ctx, st_ctx = layer_forward(xp, mod_ctx, p)
        xs, _, _, _ = layer_forward(xs, mod_lat, p, rope=rope,
                                    ctx=(cache_k[:, l], cache_v[:, l], state_lru[:, l].astype(F32)))
        ks.append(k_ctx)
        vs.append(v_ctx)
        ss.append(st_ctx.astype(xp.dtype))
    y_prompt = rmsnorm(xp, g_final)
    y_sample = rmsnorm(xs, g_final)
    new_cache_k = jnp.stack(ks, axis=1)
    new_cache_v = jnp.stack(vs, axis=1)
    new_state_lru = jnp.stack(ss, axis=1)
    return (y_prompt, y_sample, new_cache_k, new_cache_v, new_state_lru)
```

```python
import functools

import jax
import jax.numpy as jnp
from jax import lax
from jax.experimental import pallas as pl
from jax.experimental.pallas import tpu as pltpu

F32 = jnp.float32
BF16 = jnp.bfloat16
I32 = jnp.int32

D_MODEL = 2048
A_W = 512
A_GROUPS = 4
CHUNK = 128
B_W = 512
B_HEADS = 8
CONV_W = 4
RG_C = 8.0
C_W = 1024
HEAD_DIM = 64
N_HEADS = 16
N_KV = 4
GQA_G = 4
KV_W = 256
WBLK = 128
GRID_W = 64
ROPE_BASE = 10000.0
ATTN_SCALE = HEAD_DIM ** -0.5
N_EXP = 32
TOP_K = 4
D_FF = 2048
SWIGLU_LIMIT = 7.0
SWIGLU_ALPHA = 1.702
EPS = 1e-6
IN_COLS = 2 * A_W + 2 * B_W + C_W + 2 * KV_W
COL_Q = 2 * A_W + 2 * B_W
COL_K = COL_Q + C_W
COL_V = COL_K + KV_W

LANES = 128
SUBLANES = 8
VMEM_LIMIT = 56 * 1024 * 1024
TM = 256
LRU_CHUNK = 256
MOE_TILE = 1024
MOE_CHK = 256
MOE_TN = 512
GATHER_ROWS = 256
COMB_ROWS = 128


def _cparams(n_axes=1):
    return pltpu.CompilerParams(dimension_semantics=("arbitrary",) * n_axes, vmem_limit_bytes=VMEM_LIMIT)


def _mod_kernel(c_ref, w_ref, b_ref, o_ref):
    x = c_ref[...]
    s = (x * jax.nn.sigmoid(x)).astype(BF16)
    o_ref[...] = jnp.dot(s, w_ref[...].astype(BF16), preferred_element_type=F32) + b_ref[...]


def _modulation(cc, w_mod, b_mod):
    depth = w_mod.shape[0]
    r = cc.shape[0]
    tn = 1024
    return pl.pallas_call(
        _mod_kernel,
        grid=(depth, 6 * D_MODEL // tn),
        in_specs=[pl.BlockSpec((r, D_MODEL), lambda l, j: (0, 0)),
                  pl.BlockSpec((None, D_MODEL, tn), lambda l, j: (l, 0, j)),
                  pl.BlockSpec((None, 1, tn), lambda l, j: (l, 0, j))],
        out_specs=pl.BlockSpec((None, r, tn), lambda l, j: (l, 0, j)),
        out_shape=jax.ShapeDtypeStruct((depth, r, 6 * D_MODEL), F32),
        compiler_params=_cparams(2),
        name="modulation",
    )(cc, w_mod, b_mod.reshape(depth, 1, 6 * D_MODEL))


def _swap16(x):
    lane = lax.broadcasted_iota(I32, x.shape, 1)
    first = (lane % 32) < 16
    return jnp.where(first, pltpu.roll(x, LANES - 16, 1), pltpu.roll(x, 16, 1))


def _inproj_kernel(x_ref, mod_ref, g_ref, w_ref, cos_ref, sin_ref, o_ref, *, n_ctx_tiles):
    x = x_ref[...]
    y = x * lax.rsqrt(jnp.mean(x * x, axis=-1, keepdims=True) + EPS) * g_ref[...]
    h = y * (1.0 + mod_ref[1:2, :]) + mod_ref[0:1, :]
    o_ref[...] = jnp.dot(h.astype(BF16), w_ref[...], preferred_element_type=F32)

    @pl.when(pl.program_id(0) >= n_ctx_tiles)
    def _():
        cos = cos_ref[...]
        sin = sin_ref[...]
        for j in range((C_W + KV_W) // LANES):
            c0 = COL_Q + j * LANES
            blk = o_ref[:, c0:c0 + LANES]
            o_ref[:, c0:c0 + LANES] = blk * cos + _swap16(blk) * sin


def _inproj(x, mod_l, g_mix, w_in_bf, cos_t, sin_t, tile_row, n_ctx_tiles, tiles_per_seq):
    t = x.shape[0]

    def pos_map(i):
        return (jnp.maximum(i - n_ctx_tiles, 0) % tiles_per_seq, 0)

    return pl.pallas_call(
        functools.partial(_inproj_kernel, n_ctx_tiles=n_ctx_tiles),
        grid=(t // TM,),
        in_specs=[pl.BlockSpec((TM, D_MODEL), lambda i: (i, 0)),
                  pl.BlockSpec((None, 6, D_MODEL), lambda i: (tile_row(i), 0, 0)),
                  pl.BlockSpec((1, D_MODEL), lambda i: (0, 0)),
                  pl.BlockSpec((D_MODEL, IN_COLS), lambda i: (0, 0)),
                  pl.BlockSpec((TM, LANES), pos_map),
                  pl.BlockSpec((TM, LANES), pos_map)],
        out_specs=pl.BlockSpec((TM, IN_COLS), lambda i: (i, 0)),
        out_shape=jax.ShapeDtypeStruct((t, IN_COLS), F32),
        compiler_params=_cparams(1),
        name="inproj",
    )(x, mod_l, g_mix.reshape(1, D_MODEL), w_in_bf, cos_t, sin_t)


def _chunk_mlp_kernel(u_ref, v_ref, w_ref, b_ref, o_ref):
    gd = A_W // A_GROUPS
    for g in range(A_GROUPS):
        gu = jax.nn.gelu(u_ref[:, g * gd:(g + 1) * gd])
        gv = jax.nn.gelu(v_ref[:, g * gd:(g + 1) * gd]).astype(BF16)
        s = jnp.dot(w_ref[g].astype(BF16), gv, preferred_element_type=F32) + b_ref[:, g:g + 1]
        o_ref[:, g * gd:(g + 1) * gd] = gu * s


def _chunk_mlp(proj, w_sp, b_sp):
    t = proj.shape[0]
    return pl.pallas_call(
        _chunk_mlp_kernel,
        grid=(t // CHUNK,),
        in_specs=[pl.BlockSpec((CHUNK, A_W), lambda i: (i, 0)),
                  pl.BlockSpec((CHUNK, A_W), lambda i: (i, 1)),
                  pl.BlockSpec((A_GROUPS, CHUNK, CHUNK), lambda i: (0, 0, 0)),
                  pl.BlockSpec((CHUNK, A_GROUPS), lambda i: (0, 0))],
        out_specs=pl.BlockSpec((CHUNK, A_W), lambda i: (i, 0)),
        out_shape=jax.ShapeDtypeStruct((t, A_W), F32),
        compiler_params=_cparams(1),
        name="chunk_mlp",
    )(proj, proj, w_sp, b_sp.T)


def _softplus(z):
    return jnp.maximum(z, 0.0) + jnp.log1p(jnp.exp(-jnp.abs(z)))


def _expm1(x):
    u = jnp.exp(x)
    um1 = u - 1.0
    safe = jnp.logical_and(um1 != 0.0, um1 != -1.0)
    return jnp.where(safe, um1 * x / jnp.log(jnp.where(safe, u, 2.0)), jnp.where(um1 == 0.0, x, -1.0))


def _lru_kernel(xr_ref, gr_ref, h0_ref, cw_ref, cb_ref, wr_ref, br_ref, wi_ref, bi_ref, lam_ref,
                y_ref, st_ref, a_s, b_s, *, seq_len):
    ch = LRU_CHUNK
    nch = seq_len // ch
    ngrp = ch // SUBLANES
    row = lax.broadcasted_iota(I32, (SUBLANES, B_W), 0)

    for d in (0, 1):
        sp = _softplus(-lam_ref[d:d + 1, :])
        cw = cw_ref[d]
        cb = cb_ref[d:d + 1, :]
        br = br_ref[d:d + 1, :]
        bi = bi_ref[d:d + 1, :]

        def chunk_body(ci, h, d=d, sp=sp, cw=cw, cb=cb, br=br, bi=bi):
            c = ci if d == 0 else nch - 1 - ci
            r0 = pl.multiple_of(c * ch, ch)
            xc = xr_ref[pl.ds(r0, ch), :]
            if d == 0:
                hs = pl.multiple_of(jnp.maximum(r0 - SUBLANES, 0), SUBLANES)
                halo = jnp.where(c > 0, xr_ref[pl.ds(hs, SUBLANES), :], 0.0)
                win = jnp.concatenate([halo, xc], axis=0)
                offs = [SUBLANES - (CONV_W - 1) + j for j in range(CONV_W)]
            else:
                hs = pl.multiple_of(jnp.minimum(r0 + ch, seq_len - SUBLANES), SUBLANES)
                halo = jnp.where(c < nch - 1, xr_ref[pl.ds(hs, SUBLANES), :], 0.0)
                win = jnp.concatenate([xc, halo], axis=0)
                offs = [CONV_W - 1 - j for j in range(CONV_W)]
            conv = cb
            for j in range(CONV_W):
                conv = conv + win[offs[j]:offs[j] + ch, :] * cw[j:j + 1, :]
            xb = conv.astype(BF16)
            r = jax.nn.sigmoid(jnp.dot(xb, wr_ref[d], preferred_element_type=F32) + br)
            ig = jax.nn.sigmoid(jnp.dot(xb, wi_ref[d], preferred_element_type=F32) + bi)
            log_a = -RG_C * r * sp
            a_s[...] = jnp.exp(log_a)
            b_s[...] = jnp.sqrt(-_expm1(2.0 * log_a)) * (ig * conv)

            def grp_body(gi, hc):
                g = gi if d == 0 else ngrp - 1 - gi
                q0 = pl.multiple_of(g * SUBLANES, SUBLANES)
                a8 = a_s[pl.ds(q0, SUBLANES), :]
                b8 = b_s[pl.ds(q0, SUBLANES), :]
                for dd in (1, 2, 4):
                    if d == 0:
                        keep = row >= dd
                        sh = dd
                    else:
                        keep = row < SUBLANES - dd
                        sh = SUBLANES - dd
                    a_sh = jnp.where(keep, pltpu.roll(a8, sh, 0), 1.0)
                    b_sh = jnp.where(keep, pltpu.roll(b8, sh, 0), 0.0)
                    b8 = a8 * b_sh + b8
                    a8 = a8 * a_sh
                h8 = a8 * hc + b8
                o0 = pl.multiple_of(r0 + q0, SUBLANES)
                if d == 0:
                    y_ref[pl.ds(o0, SUBLANES), :] = h8
                    return h8[SUBLANES - 1:SUBLANES, :]
                y_ref[pl.ds(o0, SUBLANES), :] = y_ref[pl.ds(o0, SUBLANES), :] + h8
                return h8[0:1, :]

            h = lax.fori_loop(0, ngrp, grp_body, h, unroll=2)
            if d == 1:
                y_ref[pl.ds(r0, ch), :] = jax.nn.gelu(gr_ref[pl.ds(r0, ch), :]) * y_ref[pl.ds(r0, ch), :]
            return h

        h_last = lax.fori_loop(0, nch, chunk_body, h0_ref[d:d + 1, :])
        st_ref[d:d + 1, :] = h_last


def _lru(proj, h0, p, row0, seq_len):
    n_seq = h0.shape[0]
    blk0 = row0 // seq_len
    w2 = pl.BlockSpec((2, B_W), lambda s: (0, 0))
    wbd = pl.BlockSpec((2, B_W, B_W), lambda s: (0, 0, 0))
    y, st = pl.pallas_call(
        functools.partial(_lru_kernel, seq_len=seq_len),
        grid=(n_seq,),
        in_specs=[pl.BlockSpec((seq_len, B_W), lambda s: (blk0 + s, 2)),
                  pl.BlockSpec((seq_len, B_W), lambda s: (blk0 + s, 3)),
                  pl.BlockSpec((None, 2, B_W), lambda s: (s, 0, 0)),
                  pl.BlockSpec((2, CONV_W, B_W), lambda s: (0, 0, 0)),
                  w2, wbd, w2, wbd, w2, w2],
        out_specs=[pl.BlockSpec((seq_len, B_W), lambda s: (s, 0)),
                   pl.BlockSpec((None, 2, B_W), lambda s: (s, 0, 0))],
        out_shape=[jax.ShapeDtypeStruct((n_seq * seq_len, B_W), F32), jax.ShapeDtypeStruct((n_seq, 2, B_W), F32)],
        scratch_shapes=[pltpu.VMEM((LRU_CHUNK, B_W), F32), pltpu.VMEM((LRU_CHUNK, B_W), F32)],
        compiler_params=_cparams(1),
        name=f"rglru_{seq_len}",
    )(proj, proj, h0, p["conv_w"], p["conv_b"], p["wr_bd"], p["b_rg"], p["wi_bd"], p["b_ig"], p["lam"])
    return y, st


def _attend(q_ref, pieces, sinks_ref, o_ref, nq):
    dn = (((1,), (1,)), ((), ()))
    for kv in range(N_KV):
        heads = [kv * GQA_G + g for g in range(GQA_G)]
        qg = jnp.concatenate([q_ref[:, h * HEAD_DIM:(h + 1) * HEAD_DIM] for h in heads], axis=0)
        qg = (qg * ATTN_SCALE).astype(BF16)
        sink = jnp.concatenate([jnp.full((nq, 1), sinks_ref[h], F32) for h in heads], axis=0)
        scores = []
        m = sink
        for k_fn, _, mask in pieces:
            s = lax.dot_general(qg, k_fn(kv), dn, preferred_element_type=F32)
            if mask is not None:
                s = jnp.where(mask, s, -jnp.inf)
            scores.append(s)
            m = jnp.maximum(m, jnp.max(s, axis=-1, keepdims=True))
        den = jnp.exp(sink - m)
        o = jnp.zeros((GQA_G * nq, HEAD_DIM), F32)
        for s, (_, v_fn, _) in zip(scores, pieces):
            e = jnp.exp(s - m)
            den = den + jnp.sum(e, axis=-1, keepdims=True)
            o = o + jnp.dot(e.astype(BF16), v_fn(kv), preferred_element_type=F32)
        o = o / den
        for g, h in enumerate(heads):
            o_ref[:, h * HEAD_DIM:(h + 1) * HEAD_DIM] = o[g * nq:(g + 1) * nq, :]


def _ctx_attn_kernel(sinks_ref, q_ref, k_ref, v_ref, o_ref, *, nq):
    def k_fn(kv):
        return k_ref[:, kv * HEAD_DIM:(kv + 1) * HEAD_DIM].astype(BF16)

    def v_fn(kv):
        return v_ref[:, kv * HEAD_DIM:(kv + 1) * HEAD_DIM].astype(BF16)

    _attend(q_ref, [(k_fn, v_fn, None)], sinks_ref, o_ref, nq)


def _ctx_attention(proj, sinks, n_seq, seq_len):
    smem = pl.BlockSpec(memory_space=pltpu.SMEM)
    return pl.pallas_call(
        functools.partial(_ctx_attn_kernel, nq=seq_len),
        grid=(n_seq,),
        in_specs=[smem,
                  pl.BlockSpec((seq_len, C_W), lambda s: (s, COL_Q // C_W)),
                  pl.BlockSpec((seq_len, KV_W), lambda s: (s, COL_K // KV_W)),
                  pl.BlockSpec((seq_len, KV_W), lambda s: (s, COL_V // KV_W))],
        out_specs=pl.BlockSpec((seq_len, C_W), lambda s: (s, 0)),
        out_shape=jax.ShapeDtypeStruct((n_seq * seq_len, C_W), F32),
        compiler_params=_cparams(1),
        name="ctx_attention",
    )(sinks, proj, proj, proj)


def _lat_attn_kernel(sinks_ref, q_ref, kp_ref, kc_ref, kn_ref, vp_ref, vc_ref, vn_ref, ck_ref, cv_ref,
                     o_ref, *, nblk):
    n = pl.program_id(1)
    nq = WBLK
    rows = lax.broadcasted_iota(I32, (GQA_G * nq, WBLK), 0) % nq
    cols = lax.broadcasted_iota(I32, (GQA_G * nq, WBLK), 1)
    mask_prev = jnp.logical_and(cols >= rows, n > 0)
    mask_next = jnp.logical_and(cols <= rows, n < nblk - 1)

    def loc(ref):
        return lambda kv: ref[:, kv * HEAD_DIM:(kv + 1) * HEAD_DIM].astype(BF16)

    def cache(ref):
        return lambda kv: ref[kv].astype(BF16)

    pieces = [(loc(kp_ref), loc(vp_ref), mask_prev),
              (loc(kc_ref), loc(vc_ref), None),
              (loc(kn_ref), loc(vn_ref), mask_next),
              (cache(ck_ref), cache(cv_ref), None)]
    _attend(q_ref, pieces, sinks_ref, o_ref, nq)


def _lat_attention(proj, sinks, cache_k, cache_v, layer, row0, n_seq, seq_len):
    nblk = seq_len // WBLK
    b0 = row0 // WBLK
    past = cache_k.shape[3]
    smem = pl.BlockSpec(memory_space=pltpu.SMEM)
    kcol = COL_K // KV_W
    vcol = COL_V // KV_W

    def blk(col, off):
        return pl.BlockSpec((WBLK, KV_W), lambda b, n: (b0 + b * nblk + jnp.clip(n + off, 0, nblk - 1), col))

    cache_spec = pl.BlockSpec((None, None, N_KV, past, HEAD_DIM), lambda b, n: (b, layer, 0, 0, 0))
    return pl.pallas_call(
        functools.partial(_lat_attn_kernel, nblk=nblk),
        grid=(n_seq, nblk),
        in_specs=[smem,
                  pl.BlockSpec((WBLK, C_W), lambda b, n: (b0 + b * nblk + n, COL_Q // C_W)),
                  blk(kcol, -1), blk(kcol, 0), blk(kcol, 1),
                  blk(vcol, -1), blk(vcol, 0), blk(vcol, 1),
                  cache_spec, cache_spec],
        out_specs=pl.BlockSpec((WBLK, C_W), lambda b, n: (b * nblk + n, 0)),
        out_shape=jax.ShapeDtypeStruct((n_seq * seq_len, C_W), F32),
        compiler_params=_cparams(2),
        name="lat_attention",
    )(sinks, proj, proj, proj, proj, proj, proj, proj, cache_k, cache_v)


def _outproj_kernel(a_ref, bc_ref, bl_ref, cc_ref, cl_ref, x_ref, mod_ref, w_ref, g_ref, wrh_ref, wrl_ref, br_ref,
                    xo_ref, h2_ref, te_ref, tg_ref, rk_ref, cnt_ref, carry, *, n_ctx_tiles):
    i = pl.program_id(0)
    is_ctx = i < n_ctx_tiles
    b_mix = jnp.where(is_ctx, bc_ref[...], bl_ref[...])
    c_mix = jnp.where(is_ctx, cc_ref[...], cl_ref[...])

    @pl.when(i == 0)
    def _():
        carry[...] = jnp.zeros_like(carry)

    mixed = jnp.concatenate([a_ref[...], b_mix, c_mix], axis=-1).astype(BF16)
    mix = jnp.dot(mixed, w_ref[...], preferred_element_type=F32)
    x = x_ref[...] + mod_ref[2:3, :] * mix
    xo_ref[...] = x
    y = x * lax.rsqrt(jnp.mean(x * x, axis=-1, keepdims=True) + EPS) * g_ref[...]
    h2 = y * (1.0 + mod_ref[4:5, :]) + mod_ref[3:4, :]
    h2_ref[...] = h2

    dn = (((1,), (1,)), ((), ()))
    h_hi = h2.astype(BF16)
    h_lo = (h2 - h_hi.astype(F32)).astype(BF16)
    wh = wrh_ref[...]
    logits = (lax.dot_general(wh, h_hi, dn, preferred_element_type=F32)
              + lax.dot_general(wh, h_lo, dn, preferred_element_type=F32)
              + lax.dot_general(wrl_ref[...], h_hi, dn, preferred_element_type=F32)) + br_ref[...]

    eidx = lax.broadcasted_iota(I32, (N_EXP, TM), 0).astype(F32)
    l = logits
    vals, hots = [], []
    for k in range(TOP_K):
        m = jnp.max(l, axis=0, keepdims=True)
        sel = jnp.min(jnp.where(l == m, eidx, float(N_EXP)), axis=0, keepdims=True)
        hot = eidx == sel
        vals.append(m)
        hots.append(hot)
        te_ref[k:k + 1, :] = sel.astype(I32)
        l = jnp.where(hot, -jnp.inf, l)
    exps = [jnp.exp(v - vals[0]) for v in vals]
    den = exps[0] + exps[1] + exps[2] + exps[3]
    for k in range(TOP_K):
        tg_ref[k:k + 1, :] = exps[k] / den

    any_hot = jnp.logical_or(jnp.logical_or(hots[0], hots[1]), jnp.logical_or(hots[2], hots[3]))
    hot_f = any_hot.astype(F32)
    tri = (lax.broadcasted_iota(I32, (TM, TM), 0) < lax.broadcasted_iota(I32, (TM, TM), 1)).astype(BF16)
    pos = jnp.dot(hot_f.astype(BF16), tri, preferred_element_type=F32) + carry[...]
    for k in range(TOP_K):
        rk_ref[k:k + 1, :] = jnp.sum(jnp.where(hots[k], pos, 0.0), axis=0, keepdims=True).astype(I32)
    carry[...] = carry[...] + jnp.sum(hot_f, axis=1, keepdims=True)
    cnt_ref[...] = carry[...].astype(I32)


def _outproj(a_out, b_ctx, b_lat, c_ctx, c_lat, x, mod_l, w_out_bf, g_ffn, w_router, b_router, tile_row,
             n_ctx_tiles):
    t = x.shape[0]

    def ctx_map(i):
        return (jnp.minimum(i, n_ctx_tiles - 1), 0)

    def lat_map(i):
        return (jnp.maximum(i - n_ctx_tiles, 0), 0)

    wr_t = w_router.T
    wr_hi = wr_t.astype(BF16)
    wr_lo = (wr_t - wr_hi.astype(F32)).astype(BF16)
    row4 = pl.BlockSpec((TOP_K, TM), lambda i: (0, i))
    return pl.pallas_call(
        functools.partial(_outproj_kernel, n_ctx_tiles=n_ctx_tiles),
        grid=(t // TM,),
        in_specs=[pl.BlockSpec((TM, A_W), lambda i: (i, 0)),
                  pl.BlockSpec((TM, B_W), ctx_map),
                  pl.BlockSpec((TM, B_W), lat_map),
                  pl.BlockSpec((TM, C_W), ctx_map),
                  pl.BlockSpec((TM, C_W), lat_map),
                  pl.BlockSpec((TM, D_MODEL), lambda i: (i, 0)),
                  pl.BlockSpec((None, 6, D_MODEL), lambda i: (tile_row(i), 0, 0)),
                  pl.BlockSpec((D_MODEL, D_MODEL), lambda i: (0, 0)),
                  pl.BlockSpec((1, D_MODEL), lambda i: (0, 0)),
                  pl.BlockSpec((N_EXP, D_MODEL), lambda i: (0, 0)),
                  pl.BlockSpec((N_EXP, D_MODEL), lambda i: (0, 0)),
                  pl.BlockSpec((N_EXP, 1), lambda i: (0, 0))],
        out_specs=[pl.BlockSpec((TM, D_MODEL), lambda i: (i, 0)),
                   pl.BlockSpec((TM, D_MODEL), lambda i: (i, 0)),
                   row4, row4, row4,
                   pl.BlockSpec((N_EXP, 1), lambda i: (0, 0))],
        out_shape=[jax.ShapeDtypeStruct((t, D_MODEL), F32),
                   jax.ShapeDtypeStruct((t, D_MODEL), F32),
                   jax.ShapeDtypeStruct((TOP_K, t), I32),
                   jax.ShapeDtypeStruct((TOP_K, t), F32),
                   jax.ShapeDtypeStruct((TOP_K, t), I32),
                   jax.ShapeDtypeStruct((N_EXP, 1), I32)],
        scratch_shapes=[pltpu.VMEM((N_EXP, 1), F32)],
        compiler_params=_cparams(1),
        name="outproj_router",
    )(a_out, b_ctx, b_lat, c_ctx, c_lat, x, mod_l, w_out_bf, g_ffn.reshape(1, D_MODEL), wr_hi, wr_lo,
      b_router.reshape(N_EXP, 1))


def _route_tables(counts, top_e, rank, t):
    r_tot = t * TOP_K + N_EXP * MOE_TILE
    n_tiles = r_tot // MOE_TILE
    n_steps = (D_FF // MOE_TN) * n_tiles
    chunks_per_tile = MOE_TILE // MOE_CHK
    nn = D_FF // MOE_TN

    ntile_e = (counts + MOE_TILE - 1) // MOE_TILE
    tile_end = jnp.cumsum(ntile_e)
    tile_start = tile_end - ntile_e
    reg_start = tile_start * MOE_TILE

    dest = reg_start[top_e] + rank
    tok = jnp.tile(jnp.arange(t, dtype=I32), TOP_K)
    slot_tok = jnp.zeros((r_tot,), I32).at[dest.reshape(-1)].set(tok)

    tidx = jnp.arange(n_tiles, dtype=I32)
    te = jnp.sum(tile_end[None, :] <= tidx[:, None], axis=1).astype(I32)
    te_c = jnp.minimum(te, N_EXP - 1)
    left = counts[te_c] - (tidx - tile_start[te_c]) * MOE_TILE
    nch_tile = jnp.where(te < N_EXP, jnp.clip((left + MOE_CHK - 1) // MOE_CHK, 0, chunks_per_tile), 0).astype(I32)

    steps_e = nn * ntile_e
    s_end = jnp.cumsum(steps_e)
    s_start = s_end - steps_e
    total = s_end[-1]
    sidx = jnp.arange(n_steps, dtype=I32)
    valid = sidx < total
    last = jnp.maximum(total - 1, 0)
    sidx_c = jnp.where(valid, sidx, last)
    se = jnp.minimum(jnp.sum(s_end[None, :] <= sidx_c[:, None], axis=1), N_EXP - 1).astype(I32)
    loc = sidx_c - s_start[se]
    nt = jnp.maximum(ntile_e[se], 1)
    sj = loc % nt
    used = tile_end[-1]
    n_unused = jnp.maximum(n_tiles - used, 1)
    extra = jnp.maximum(sidx - total, 0)
    sn = jnp.where(valid, loc // nt, extra // n_unused).astype(I32)
    stile = jnp.clip(jnp.where(valid, tile_start[se] + sj, used + extra % n_unused), 0, n_tiles - 1).astype(I32)
    snch = jnp.where(valid, nch_tile[stile], 0).astype(I32)
    first = jnp.logical_and(valid, sj == 0).astype(I32)

    blk_per_tile = MOE_TILE // GATHER_ROWS
    bidx = jnp.arange(r_tot // GATHER_ROWS, dtype=I32)
    rows_valid = nch_tile[bidx // blk_per_tile] * MOE_CHK
    blk_valid = ((bidx % blk_per_tile) * GATHER_ROWS < rows_valid).astype(I32)
    sched = (se, sn, stile, snch, first)
    return dest, slot_tok, blk_valid, sched


def _gather_kernel(valid_ref, idx_cur, idx_nxt, src_hbm, o_ref, buf, sem, *, nblk):
    i = pl.program_id(0)

    def row_copy(idx_ref, r, slot):
        return pltpu.make_async_copy(src_hbm.at[pl.ds(idx_ref[0, r], 1)], buf.at[slot, pl.ds(r, 1)], sem.at[slot])

    def issue(idx_ref, slot):
        def body(r, carry):
            row_copy(idx_ref, r, slot).start()
            return carry
        lax.fori_loop(0, GATHER_ROWS, body, 0, unroll=8)

    @pl.when(jnp.logical_and(i == 0, valid_ref[0] > 0))
    def _():
        issue(idx_cur, 0)

    nxt = jnp.minimum(i + 1, nblk - 1)

    @pl.when(jnp.logical_and(i + 1 < nblk, valid_ref[nxt] > 0))
    def _():
        issue(idx_nxt, (i + 1) % 2)

    slot = i % 2

    @pl.when(valid_ref[i] > 0)
    def _():
        def body(r, carry):
            row_copy(idx_cur, r, slot).wait()
            return carry
        lax.fori_loop(0, GATHER_ROWS, body, 0, unroll=8)
        o_ref[...] = buf[slot].astype(BF16)

    @pl.when(valid_ref[i] == 0)
    def _():
        o_ref[...] = jnp.zeros_like(o_ref)


def _gather_rows(h2, slot_tok, blk_valid):
    r_tot = slot_tok.shape[0]
    nblk = r_tot // GATHER_ROWS
    idx3 = slot_tok.reshape(nblk, 1, GATHER_ROWS)
    grid_spec = pltpu.PrefetchScalarGridSpec(
        num_scalar_prefetch=1,
        grid=(nblk,),
        in_specs=[pl.BlockSpec((None, 1, GATHER_ROWS), lambda i, v: (i, 0, 0), memory_space=pltpu.SMEM),
                  pl.BlockSpec((None, 1, GATHER_ROWS), lambda i, v: (jnp.minimum(i + 1, nblk - 1), 0, 0),
                               memory_space=pltpu.SMEM),
                  pl.BlockSpec(memory_space=pl.ANY)],
        out_specs=pl.BlockSpec((GATHER_ROWS, D_MODEL), lambda i, v: (i, 0)),
        scratch_shapes=[pltpu.VMEM((2, GATHER_ROWS, D_MODEL), F32), pltpu.SemaphoreType.DMA((2,))],
    )
    return pl.pallas_call(
        functools.partial(_gather_kernel, nblk=nblk),
        grid_spec=grid_spec,
        out_shape=jax.ShapeDtypeStruct((r_tot, D_MODEL), BF16),
        compiler_params=_cparams(1),
        name="moe_gather",
    )(blk_valid, idx3, idx3, h2)


def _gmm_gu_kernel(e_ref, n_ref, t_ref, nch_ref, first_ref, x_ref, wg_ref, wu_ref, bg_ref, bu_ref,
                   o_ref, wg_s, wu_s):
    i = pl.program_id(0)
    del e_ref, n_ref, t_ref

    @pl.when(first_ref[i] > 0)
    def _():
        wg_s[...] = wg_ref[...].astype(BF16)
        wu_s[...] = wu_ref[...].astype(BF16)

    nch = nch_ref[i]

    def body(c, carry):
        r0 = pl.multiple_of(c * MOE_CHK, MOE_CHK)
        xs = x_ref[pl.ds(r0, MOE_CHK), :]
        g = jnp.dot(xs, wg_s[...], preferred_element_type=F32) + bg_ref[...]
        u = jnp.dot(xs, wu_s[...], preferred_element_type=F32) + bu_ref[...]
        g = jnp.minimum(g, SWIGLU_LIMIT)
        u = jnp.clip(u, -SWIGLU_LIMIT, SWIGLU_LIMIT)
        act = (u + 1.0) * (g * jax.nn.sigmoid(SWIGLU_ALPHA * g))
        o_ref[pl.ds(r0, MOE_CHK), :] = act.astype(BF16)
        return carry

    lax.fori_loop(0, nch, body, 0)

    def zero(c, carry):
        r0 = pl.multiple_of(c * MOE_CHK, MOE_CHK)
        o_ref[pl.ds(r0, MOE_CHK), :] = jnp.zeros((MOE_CHK, MOE_TN), BF16)
        return carry

    lax.fori_loop(nch, MOE_TILE // MOE_CHK, zero, 0)


def _gmm_down_kernel(e_ref, n_ref, t_ref, nch_ref, first_ref, a_ref, w_ref, b_ref, o_ref, w_s):
    i = pl.program_id(0)
    del e_ref, n_ref, t_ref

    @pl.when(first_ref[i] > 0)
    def _():
        w_s[...] = w_ref[...].astype(BF16)

    nch = nch_ref[i]

    def body(c, carry):
        r0 = pl.multiple_of(c * MOE_CHK, MOE_CHK)
        y = jnp.dot(a_ref[pl.ds(r0, MOE_CHK), :], w_s[...], preferred_element_type=F32) + b_ref[...]
        o_ref[pl.ds(r0, MOE_CHK), :] = y
        return carry

    lax.fori_loop(0, nch, body, 0)

    def zero(c, carry):
        r0 = pl.multiple_of(c * MOE_CHK, MOE_CHK)
        o_ref[pl.ds(r0, MOE_CHK), :] = jnp.zeros((MOE_CHK, MOE_TN), F32)
        return carry

    lax.fori_loop(nch, MOE_TILE // MOE_CHK, zero, 0)


def _moe_experts(x_sorted, sched, w_gu, b_gu, w_down, b_down, layer):
    r_tot = x_sorted.shape[0]
    n_steps = sched[0].shape[0]
    nn = D_FF // MOE_TN
    depth = w_gu.shape[0]
    b_gu4 = b_gu.reshape(depth, N_EXP, 1, 2 * D_FF)
    b_dn4 = b_down.reshape(depth, N_EXP, 1, D_MODEL)

    def wspec(rows, off):
        return pl.BlockSpec((None, None, rows, MOE_TN), lambda i, e, n, t, c, f: (layer, e[i], 0, n[i] + off))

    gu_spec = pltpu.PrefetchScalarGridSpec(
        num_scalar_prefetch=5,
        grid=(n_steps,),
        in_specs=[pl.BlockSpec((MOE_TILE, D_MODEL), lambda i, e, n, t, c, f: (t[i], 0)),
                  wspec(D_MODEL, 0), wspec(D_MODEL, nn), wspec(1, 0), wspec(1, nn)],
        out_specs=pl.BlockSpec((MOE_TILE, MOE_TN), lambda i, e, n, t, c, f: (t[i], n[i])),
        scratch_shapes=[pltpu.VMEM((D_MODEL, MOE_TN), BF16), pltpu.VMEM((D_MODEL, MOE_TN), BF16)],
    )
    act = pl.pallas_call(
        _gmm_gu_kernel,
        grid_spec=gu_spec,
        out_shape=jax.ShapeDtypeStruct((r_tot, D_FF), BF16),
        compiler_params=_cparams(1),
        name="moe_gate_up",
    )(*sched, x_sorted, w_gu, w_gu, b_gu4, b_gu4)

    dn_spec = pltpu.PrefetchScalarGridSpec(
        num_scalar_prefetch=5,
        grid=(n_steps,),
        in_specs=[pl.BlockSpec((MOE_TILE, D_FF), lambda i, e, n, t, c, f: (t[i], 0)),
                  wspec(D_FF, 0), wspec(1, 0)],
        out_specs=pl.BlockSpec((MOE_TILE, MOE_TN), lambda i, e, n, t, c, f: (t[i], n[i])),
        scratch_shapes=[pltpu.VMEM((D_FF, MOE_TN), BF16)],
    )
    return pl.pallas_call(
        _gmm_down_kernel,
        grid_spec=dn_spec,
        out_shape=jax.ShapeDtypeStruct((r_tot, D_MODEL), F32),
        compiler_params=_cparams(1),
        name="moe_down",
    )(*sched, act, w_down, b_dn4)


def _combine_kernel(idx_cur, idx_nxt, y_hbm, x_ref, mod_ref, g_ref, o_ref, buf, sem, *, nblk):
    i = pl.program_id(0)

    def row_copy(idx_ref, k, r, slot):
        return pltpu.make_async_copy(y_hbm.at[pl.ds(idx_ref[k, r], 1)], buf.at[slot, k, pl.ds(r, 1)], sem.at[slot])

    def issue(idx_ref, slot):
        def body(r, carry):
            for k in range(TOP_K):
                row_copy(idx_ref, k, r, slot).start()
            return carry
        lax.fori_loop(0, COMB_ROWS, body, 0, unroll=4)

    @pl.when(i == 0)
    def _():
        issue(idx_cur, 0)

    @pl.when(i + 1 < nblk)
    def _():
        issue(idx_nxt, (i + 1) % 2)

    slot = i % 2

    def wbody(r, carry):
        for k in range(TOP_K):
            row_copy(idx_cur, k, r, slot).wait()
        return carry
    lax.fori_loop(0, COMB_ROWS, wbody, 0, unroll=4)

    g = g_ref[...]
    moe = buf[slot, 0] * g[:, 0:1]
    for k in range(1, TOP_K):
        moe = moe + buf[slot, k] * g[:, k:k + 1]
    o_ref[...] = x_ref[...] + mod_ref[5:6, :] * moe


def _combine(y_sorted, dest, gates_t, x_mid, mod_l, blk_row):
    t = x_mid.shape[0]
    nblk = t // COMB_ROWS
    idx3 = dest.reshape(TOP_K, nblk, COMB_ROWS).transpose(1, 0, 2)
    gates = gates_t.T
    return pl.pallas_call(
        functools.partial(_combine_kernel, nblk=nblk),
        grid=(nblk,),
        in_specs=[pl.BlockSpec((None, TOP_K, COMB_ROWS), lambda i: (i, 0, 0), memory_space=pltpu.SMEM),
                  pl.BlockSpec((None, TOP_K, COMB_ROWS), lambda i: (jnp.minimum(i + 1, nblk - 1), 0, 0),
                               memory_space=pltpu.SMEM),
                  pl.BlockSpec(memory_space=pl.ANY),
                  pl.BlockSpec((COMB_ROWS, D_MODEL), lambda i: (i, 0)),
                  pl.BlockSpec((None, 6, D_MODEL), lambda i: (blk_row(i), 0, 0)),
                  pl.BlockSpec((COMB_ROWS, TOP_K), lambda i: (i, 0))],
        out_specs=pl.BlockSpec((COMB_ROWS, D_MODEL), lambda i: (i, 0)),
        out_shape=jax.ShapeDtypeStruct((t, D_MODEL), F32),
        scratch_shapes=[pltpu.VMEM((2, TOP_K, COMB_ROWS, D_MODEL), F32), pltpu.SemaphoreType.DMA((2,))],
        compiler_params=_cparams(1),
        name="moe_combine",
    )(idx3, idx3, y_sorted, x_mid, mod_l, gates)


def _final_norm_kernel(x_ref, g_ref, o_ref):
    x = x_ref[...]
    o_ref[...] = x * lax.rsqrt(jnp.mean(x * x, axis=-1, keepdims=True) + EPS) * g_ref[...]


def _final_norm(x, g):
    t = x.shape[0]
    return pl.pallas_call(
        _final_norm_kernel,
        grid=(t // TM,),
        in_specs=[pl.BlockSpec((TM, D_MODEL), lambda i: (i, 0)), pl.BlockSpec((1, D_MODEL), lambda i: (0, 0))],
        out_specs=pl.BlockSpec((TM, D_MODEL), lambda i: (i, 0)),
        out_shape=jax.ShapeDtypeStruct((t, D_MODEL), F32),
        compiler_params=_cparams(1),
        name="final_norm",
    )(x, g.reshape(1, D_MODEL))


def _rope_tables(n):
    rows = n // GRID_W
    row = jnp.repeat(jnp.arange(rows), GRID_W).astype(F32)
    col = jnp.tile(jnp.arange(GRID_W), rows).astype(F32)
    n_freq = HEAD_DIM // 4
    inv = ROPE_BASE ** (-jnp.arange(n_freq, dtype=F32) / n_freq)
    ang = jnp.stack([row[:, None] * inv, col[:, None] * inv], axis=1)
    cos, sin = jnp.cos(ang), jnp.sin(ang)
    c64 = jnp.stack([cos, cos], axis=2).reshape(n, HEAD_DIM)
    s64 = jnp.stack([-sin, sin], axis=2).reshape(n, HEAD_DIM)
    rep = LANES // HEAD_DIM
    return jnp.tile(c64, (1, rep)), jnp.tile(s64, (1, rep))


def _block_diag(w):
    eye = jnp.eye(B_HEADS, dtype=w.dtype)
    hd = B_W // B_HEADS
    full = w[:, :, :, None, :] * eye[None, :, None, :, None]
    return full.reshape(2, B_W, B_W).astype(BF16)


def kernel(x_prompt, x_sample, cache_k, cache_v, state_lru, c, c_ctx, w_mod, b_mod, g_mix, w_in, w_sp, b_sp, conv_w, conv_b, w_rg, b_rg, w_ig, b_ig, lru_lambda, sinks, w_out, g_ffn, w_router, b_router, w_gu, b_gu, w_down, b_down, g_final):
    nb_c, len_c, _ = x_prompt.shape
    nb_l, len_l, _ = x_sample.shape
    depth = w_mod.shape[0]
    tc, tl = nb_c * len_c, nb_l * len_l
    t = tc + tl
    assert len_c % TM == 0 and len_l % TM == 0 and len_l % GRID_W == 0
    assert len_c % LRU_CHUNK == 0 and len_l % LRU_CHUNK == 0 and tc % len_l == 0

    x = jnp.concatenate([x_prompt.reshape(tc, D_MODEL), x_sample.reshape(tl, D_MODEL)], axis=0)
    n_rows = 1 + nb_l
    r_pad = -(-n_rows // SUBLANES) * SUBLANES
    cc = jnp.zeros((r_pad, D_MODEL), F32).at[0].set(c_ctx).at[1:n_rows].set(c)
    mod = _modulation(cc, w_mod, b_mod).reshape(depth, r_pad, 6, D_MODEL)

    def row_map(rows_per_blk):
        n_ctx_blk = tc // rows_per_blk
        per_seq = len_l // rows_per_blk
        return lambda i: jnp.where(i < n_ctx_blk, 0, 1 + (jnp.maximum(i - n_ctx_blk, 0) // per_seq))

    tile_row = row_map(TM)
    cos_t, sin_t = _rope_tables(len_l)
    w_in_bf = w_in.astype(BF16)
    w_out_bf = w_out.astype(BF16)
    h0_ctx = jnp.zeros((nb_c, 2, B_W), F32)

    ks, vs, ss = [], [], []
    for l in range(depth):
        mod_l = mod[l]
        proj = _inproj(x, mod_l, g_mix[l], w_in_bf[l], cos_t, sin_t, tile_row, tc // TM, len_l // TM)
        a_out = _chunk_mlp(proj, w_sp[l], b_sp[l])
        lru_p = {"conv_w": conv_w[l], "conv_b": conv_b[l], "wr_bd": _block_diag(w_rg[l]), "b_rg": b_rg[l],
                 "wi_bd": _block_diag(w_ig[l]), "b_ig": b_ig[l], "lam": lru_lambda[l]}
        b_ctx, st_ctx = _lru(proj, h0_ctx, lru_p, 0, len_c)
        b_lat, _ = _lru(proj, state_lru[:, l].astype(F32), lru_p, tc, len_l)
        c_ctx_out = _ctx_attention(proj, sinks[l], nb_c, len_c)
        c_lat_out = _lat_attention(proj, sinks[l], cache_k, cache_v, l, tc, nb_l, len_l)
        x_mid, h2, top_e, gates_t, rank, counts = _outproj(
            a_out, b_ctx, b_lat, c_ctx_out, c_lat_out, x, mod_l, w_out_bf[l], g_ffn[l], w_router[l], b_router[l],
            tile_row, tc // TM)
        dest, slot_tok, blk_valid, sched = _route_tables(counts[:, 0], top_e, rank, t)
        x_sorted = _gather_rows(h2, slot_tok, blk_valid)
        y_sorted = _moe_experts(x_sorted, sched, w_gu, b_gu, w_down, b_down, l)
        x = _combine(y_sorted, dest, gates_t, x_mid, mod_l, row_map(COMB_ROWS))

        kctx = proj[:tc, COL_K:COL_K + KV_W].reshape(nb_c, len_c, N_KV, HEAD_DIM).transpose(0, 2, 1, 3)
        vctx = proj[:tc, COL_V:COL_V + KV_W].reshape(nb_c, len_c, N_KV, HEAD_DIM).transpose(0, 2, 1, 3)
        ks.append(kctx)
        vs.append(vctx)
        ss.append(st_ctx)

    y = _final_norm(x, g_final)
    y_prompt = y[:tc].reshape(nb_c, len_c, D_MODEL)
    y_sample = y[tc:].reshape(nb_l, len_l, D_MODEL)
    return (y_prompt, y_sample, jnp.stack(ks, axis=1), jnp.stack(vs, axis=1), jnp.stack(ss, axis=1))
```

```python
import functools

import jax
import jax.numpy as jnp
from jax import lax
from jax.experimental import pallas as pl
from jax.experimental.pallas import tpu as pltpu

F32 = jnp.float32
BF16 = jnp.bfloat16
I32 = jnp.int32

D_MODEL = 2048
A_W = 512
A_GROUPS = 4
CHUNK = 128
B_W = 512
B_HEADS = 8
CONV_W = 4
RG_C = 8.0
C_W = 1024
HEAD_DIM = 64
N_HEADS = 16
N_KV = 4
GQA_G = 4
KV_W = 256
WBLK = 128
GRID_W = 64
ROPE_BASE = 10000.0
ATTN_SCALE = HEAD_DIM ** -0.5
N_EXP = 32
TOP_K = 4
D_FF = 2048
SWIGLU_LIMIT = 7.0
SWIGLU_ALPHA = 1.702
EPS = 1e-6
IN_COLS = 2 * A_W + 2 * B_W + C_W + 2 * KV_W
COL_Q = 2 * A_W + 2 * B_W
COL_K = COL_Q + C_W
COL_V = COL_K + KV_W

LANES = 128
SUBLANES = 8
VMEM_LIMIT = 56 * 1024 * 1024
TM = 256
LRU_CHUNK = 256
MOE_TILE = 1024
MOE_CHK = 256
MOE_TN = 512
MOE_TN_DOWN = 1024
SCAT_ROWS = 128
COMB_ROWS = 128
HI_MASK = -65536


def _cparams(n_axes=1):
    return pltpu.CompilerParams(dimension_semantics=("arbitrary",) * n_axes, vmem_limit_bytes=VMEM_LIMIT)


def _mod_kernel(c_ref, w_ref, b_ref, o_ref):
    x = c_ref[...]
    s = (x * jax.nn.sigmoid(x)).astype(BF16)
    o_ref[...] = jnp.dot(s, w_ref[...].astype(BF16), preferred_element_type=F32) + b_ref[...]


def _modulation(cc, w_mod, b_mod):
    depth = w_mod.shape[0]
    r = cc.shape[0]
    tn = 1024
    return pl.pallas_call(
        _mod_kernel,
        grid=(depth, 6 * D_MODEL // tn),
        in_specs=[pl.BlockSpec((r, D_MODEL), lambda l, j: (0, 0)),
                  pl.BlockSpec((None, D_MODEL, tn), lambda l, j: (l, 0, j)),
                  pl.BlockSpec((None, 1, tn), lambda l, j: (l, 0, j))],
        out_specs=pl.BlockSpec((None, r, tn), lambda l, j: (l, 0, j)),
        out_shape=jax.ShapeDtypeStruct((depth, r, 6 * D_MODEL), F32),
        compiler_params=_cparams(2),
        name="modulation",
    )(cc, w_mod, b_mod.reshape(depth, 1, 6 * D_MODEL))


def _swap16(x):
    lane = lax.broadcasted_iota(I32, x.shape, 1)
    first = (lane % 32) < 16
    return jnp.where(first, pltpu.roll(x, LANES - 16, 1), pltpu.roll(x, 16, 1))


def _inproj_kernel(x_ref, mod_ref, g_ref, w_ref, cos_ref, sin_ref, o_ref, *, n_ctx_tiles):
    x = x_ref[...]
    y = x * lax.rsqrt(jnp.mean(x * x, axis=-1, keepdims=True) + EPS) * g_ref[...]
    h = y * (1.0 + mod_ref[1:2, :]) + mod_ref[0:1, :]
    o_ref[...] = jnp.dot(h.astype(BF16), w_ref[...], preferred_element_type=F32)

    @pl.when(pl.program_id(0) >= n_ctx_tiles)
    def _():
        cos = cos_ref[...]
        sin = sin_ref[...]
        for j in range((C_W + KV_W) // LANES):
            c0 = COL_Q + j * LANES
            blk = o_ref[:, c0:c0 + LANES]
            o_ref[:, c0:c0 + LANES] = blk * cos + _swap16(blk) * sin


def _inproj(x, mod_l, g_mix, w_in_bf, cos_t, sin_t, tile_row, n_ctx_tiles, tiles_per_seq):
    t = x.shape[0]

    def pos_map(i):
        return (jnp.maximum(i - n_ctx_tiles, 0) % tiles_per_seq, 0)

    return pl.pallas_call(
        functools.partial(_inproj_kernel, n_ctx_tiles=n_ctx_tiles),
        grid=(t // TM,),
        in_specs=[pl.BlockSpec((TM, D_MODEL), lambda i: (i, 0)),
                  pl.BlockSpec((None, 6, D_MODEL), lambda i: (tile_row(i), 0, 0)),
                  pl.BlockSpec((1, D_MODEL), lambda i: (0, 0)),
                  pl.BlockSpec((D_MODEL, IN_COLS), lambda i: (0, 0)),
                  pl.BlockSpec((TM, LANES), pos_map),
                  pl.BlockSpec((TM, LANES), pos_map)],
        out_specs=pl.BlockSpec((TM, IN_COLS), lambda i: (i, 0)),
        out_shape=jax.ShapeDtypeStruct((t, IN_COLS), F32),
        compiler_params=_cparams(1),
        name="inproj",
    )(x, mod_l, g_mix.reshape(1, D_MODEL), w_in_bf, cos_t, sin_t)


def _chunk_mlp_kernel(u_ref, v_ref, w_ref, b_ref, o_ref):
    gd = A_W // A_GROUPS
    for g in range(A_GROUPS):
        gu = jax.nn.gelu(u_ref[:, g * gd:(g + 1) * gd])
        gv = jax.nn.gelu(v_ref[:, g * gd:(g + 1) * gd]).astype(BF16)
        s = jnp.dot(w_ref[g].astype(BF16), gv, preferred_element_type=F32) + b_ref[:, g:g + 1]
        o_ref[:, g * gd:(g + 1) * gd] = gu * s


def _chunk_mlp(proj, w_sp, b_sp):
    t = proj.shape[0]
    return pl.pallas_call(
        _chunk_mlp_kernel,
        grid=(t // CHUNK,),
        in_specs=[pl.BlockSpec((CHUNK, A_W), lambda i: (i, 0)),
                  pl.BlockSpec((CHUNK, A_W), lambda i: (i, 1)),
                  pl.BlockSpec((A_GROUPS, CHUNK, CHUNK), lambda i: (0, 0, 0)),
                  pl.BlockSpec((CHUNK, A_GROUPS), lambda i: (0, 0))],
        out_specs=pl.BlockSpec((CHUNK, A_W), lambda i: (i, 0)),
        out_shape=jax.ShapeDtypeStruct((t, A_W), F32),
        compiler_params=_cparams(1),
        name="chunk_mlp",
    )(proj, proj, w_sp, b_sp.T)


def _softplus(z):
    return jnp.maximum(z, 0.0) + jnp.log1p(jnp.exp(-jnp.abs(z)))


def _expm1(x):
    u = jnp.exp(x)
    um1 = u - 1.0
    safe = jnp.logical_and(um1 != 0.0, um1 != -1.0)
    return jnp.where(safe, um1 * x / jnp.log(jnp.where(safe, u, 2.0)), jnp.where(um1 == 0.0, x, -1.0))


def _lru_kernel(xr_ref, gr_ref, h0_ref, cw_ref, cb_ref, wr_ref, br_ref, wi_ref, bi_ref, lam_ref,
                y_ref, st_ref, a_s, b_s, *, seq_len):
    ch = LRU_CHUNK
    nch = seq_len // ch
    ngrp = ch // SUBLANES
    row = lax.broadcasted_iota(I32, (SUBLANES, B_W), 0)

    for d in (0, 1):
        sp = _softplus(-lam_ref[d:d + 1, :])
        cw = cw_ref[d]
        cb = cb_ref[d:d + 1, :]
        br = br_ref[d:d + 1, :]
        bi = bi_ref[d:d + 1, :]

        def chunk_body(ci, h, d=d, sp=sp, cw=cw, cb=cb, br=br, bi=bi):
            c = ci if d == 0 else nch - 1 - ci
            r0 = pl.multiple_of(c * ch, ch)
            xc = xr_ref[pl.ds(r0, ch), :]
            if d == 0:
                hs = pl.multiple_of(jnp.maximum(r0 - SUBLANES, 0), SUBLANES)
                halo = jnp.where(c > 0, xr_ref[pl.ds(hs, SUBLANES), :], 0.0)
                win = jnp.concatenate([halo, xc], axis=0)
                offs = [SUBLANES - (CONV_W - 1) + j for j in range(CONV_W)]
            else:
                hs = pl.multiple_of(jnp.minimum(r0 + ch, seq_len - SUBLANES), SUBLANES)
                halo = jnp.where(c < nch - 1, xr_ref[pl.ds(hs, SUBLANES), :], 0.0)
                win = jnp.concatenate([xc, halo], axis=0)
                offs = [CONV_W - 1 - j for j in range(CONV_W)]
            conv = cb
            for j in range(CONV_W):
                conv = conv + win[offs[j]:offs[j] + ch, :] * cw[j:j + 1, :]
            xb = conv.astype(BF16)
            r = jax.nn.sigmoid(jnp.dot(xb, wr_ref[d], preferred_element_type=F32) + br)
            ig = jax.nn.sigmoid(jnp.dot(xb, wi_ref[d], preferred_element_type=F32) + bi)
            log_a = -RG_C * r * sp
            a_s[...] = jnp.exp(log_a)
            b_s[...] = jnp.sqrt(-_expm1(2.0 * log_a)) * (ig * conv)

            def grp_body(gi, hc):
                g = gi if d == 0 else ngrp - 1 - gi
                q0 = pl.multiple_of(g * SUBLANES, SUBLANES)
                a8 = a_s[pl.ds(q0, SUBLANES), :]
                b8 = b_s[pl.ds(q0, SUBLANES), :]
                for dd in (1, 2, 4):
                    if d == 0:
                        keep = row >= dd
                        sh = dd
                    else:
                        keep = row < SUBLANES - dd
                        sh = SUBLANES - dd
                    a_sh = jnp.where(keep, pltpu.roll(a8, sh, 0), 1.0)
                    b_sh = jnp.where(keep, pltpu.roll(b8, sh, 0), 0.0)
                    b8 = a8 * b_sh + b8
                    a8 = a8 * a_sh
                h8 = a8 * hc + b8
                o0 = pl.multiple_of(r0 + q0, SUBLANES)
                if d == 0:
                    y_ref[pl.ds(o0, SUBLANES), :] = h8
                    return h8[SUBLANES - 1:SUBLANES, :]
                y_ref[pl.ds(o0, SUBLANES), :] = y_ref[pl.ds(o0, SUBLANES), :] + h8
                return h8[0:1, :]

            h = lax.fori_loop(0, ngrp, grp_body, h, unroll=2)
            if d == 1:
                y_ref[pl.ds(r0, ch), :] = jax.nn.gelu(gr_ref[pl.ds(r0, ch), :]) * y_ref[pl.ds(r0, ch), :]
            return h

        h_last = lax.fori_loop(0, nch, chunk_body, h0_ref[d:d + 1, :])
        st_ref[d:d + 1, :] = h_last


def _lru(proj, h0, p, row0, seq_len):
    n_seq = h0.shape[0]
    blk0 = row0 // seq_len
    w2 = pl.BlockSpec((2, B_W), lambda s: (0, 0))
    wbd = pl.BlockSpec((2, B_W, B_W), lambda s: (0, 0, 0))
    y, st = pl.pallas_call(
        functools.partial(_lru_kernel, seq_len=seq_len),
        grid=(n_seq,),
        in_specs=[pl.BlockSpec((seq_len, B_W), lambda s: (blk0 + s, 2)),
                  pl.BlockSpec((seq_len, B_W), lambda s: (blk0 + s, 3)),
                  pl.BlockSpec((None, 2, B_W), lambda s: (s, 0, 0)),
                  pl.BlockSpec((2, CONV_W, B_W), lambda s: (0, 0, 0)),
                  w2, wbd, w2, wbd, w2, w2],
        out_specs=[pl.BlockSpec((seq_len, B_W), lambda s: (s, 0)),
                   pl.BlockSpec((None, 2, B_W), lambda s: (s, 0, 0))],
        out_shape=[jax.ShapeDtypeStruct((n_seq * seq_len, B_W), F32), jax.ShapeDtypeStruct((n_seq, 2, B_W), F32)],
        scratch_shapes=[pltpu.VMEM((LRU_CHUNK, B_W), F32), pltpu.VMEM((LRU_CHUNK, B_W), F32)],
        compiler_params=_cparams(1),
        name=f"rglru_{seq_len}",
    )(proj, proj, h0, p["conv_w"], p["conv_b"], p["wr_bd"], p["b_rg"], p["wi_bd"], p["b_ig"], p["lam"])
    return y, st


def _attend(q_ref, pieces, sinks_ref, o_ref, nq):
    dn = (((1,), (1,)), ((), ()))
    bias = None
    if any(p[2] is not None for p in pieces):
        parts = [jnp.zeros((GQA_G * nq, p[3]), F32) if p[2] is None else jnp.where(p[2], 0.0, -jnp.inf)
                 for p in pieces]
        bias = jnp.concatenate(parts, axis=1)
    for kv in range(N_KV):
        heads = [kv * GQA_G + g for g in range(GQA_G)]
        qg = jnp.concatenate([q_ref[:, h * HEAD_DIM:(h + 1) * HEAD_DIM] for h in heads], axis=0)
        qg = (qg * ATTN_SCALE).astype(BF16)
        sink = jnp.concatenate([jnp.full((nq, 1), sinks_ref[h], F32) for h in heads], axis=0)
        k_all = jnp.concatenate([p[0](kv) for p in pieces], axis=0)
        v_all = jnp.concatenate([p[1](kv) for p in pieces], axis=0)
        s = lax.dot_general(qg, k_all, dn, preferred_element_type=F32)
        if bias is not None:
            s = s + bias
        m = jnp.maximum(sink, jnp.max(s, axis=-1, keepdims=True))
        e = jnp.exp(s - m)
        den = jnp.exp(sink - m) + jnp.sum(e, axis=-1, keepdims=True)
        o = jnp.dot(e.astype(BF16), v_all, preferred_element_type=F32) / den
        for g, h in enumerate(heads):
            o_ref[:, h * HEAD_DIM:(h + 1) * HEAD_DIM] = o[g * nq:(g + 1) * nq, :]


def _ctx_attn_kernel(sinks_ref, q_ref, k_ref, v_ref, o_ref, *, nq):
    def k_fn(kv):
        return k_ref[:, kv * HEAD_DIM:(kv + 1) * HEAD_DIM].astype(BF16)

    def v_fn(kv):
        return v_ref[:, kv * HEAD_DIM:(kv + 1) * HEAD_DIM].astype(BF16)

    _attend(q_ref, [(k_fn, v_fn, None, nq)], sinks_ref, o_ref, nq)


def _ctx_attention(proj, sinks, n_seq, seq_len):
    smem = pl.BlockSpec(memory_space=pltpu.SMEM)
    return pl.pallas_call(
        functools.partial(_ctx_attn_kernel, nq=seq_len),
        grid=(n_seq,),
        in_specs=[smem,
                  pl.BlockSpec((seq_len, C_W), lambda s: (s, COL_Q // C_W)),
                  pl.BlockSpec((seq_len, KV_W), lambda s: (s, COL_K // KV_W)),
                  pl.BlockSpec((seq_len, KV_W), lambda s: (s, COL_V // KV_W))],
        out_specs=pl.BlockSpec((seq_len, C_W), lambda s: (s, 0)),
        out_shape=jax.ShapeDtypeStruct((n_seq * seq_len, C_W), F32),
        compiler_params=_cparams(1),
        name="ctx_attention",
    )(sinks, proj, proj, proj)


def _lat_attn_kernel(sinks_ref, q_ref, kp_ref, kc_ref, kn_ref, vp_ref, vc_ref, vn_ref, ck_ref, cv_ref,
                     o_ref, *, nblk):
    n = pl.program_id(1)
    nq = WBLK
    rows = lax.broadcasted_iota(I32, (GQA_G * nq, WBLK), 0) % nq
    cols = lax.broadcasted_iota(I32, (GQA_G * nq, WBLK), 1)
    mask_prev = jnp.logical_and(cols >= rows, n > 0)
    mask_next = jnp.logical_and(cols <= rows, n < nblk - 1)

    def loc(ref):
        return lambda kv: ref[:, kv * HEAD_DIM:(kv + 1) * HEAD_DIM].astype(BF16)

    def cache(ref):
        return lambda kv: ref[kv].astype(BF16)

    pieces = [(loc(kp_ref), loc(vp_ref), mask_prev, WBLK),
              (loc(kc_ref), loc(vc_ref), None, WBLK),
              (loc(kn_ref), loc(vn_ref), mask_next, WBLK),
              (cache(ck_ref), cache(cv_ref), None, ck_ref.shape[1])]
    _attend(q_ref, pieces, sinks_ref, o_ref, nq)


def _lat_attention(proj, sinks, cache_k, cache_v, layer, row0, n_seq, seq_len):
    nblk = seq_len // WBLK
    b0 = row0 // WBLK
    past = cache_k.shape[3]
    smem = pl.BlockSpec(memory_space=pltpu.SMEM)
    kcol = COL_K // KV_W
    vcol = COL_V // KV_W

    def blk(col, off):
        return pl.BlockSpec((WBLK, KV_W), lambda b, n: (b0 + b * nblk + jnp.clip(n + off, 0, nblk - 1), col))

    cache_spec = pl.BlockSpec((None, None, N_KV, past, HEAD_DIM), lambda b, n: (b, layer, 0, 0, 0))
    return pl.pallas_call(
        functools.partial(_lat_attn_kernel, nblk=nblk),
        grid=(n_seq, nblk),
        in_specs=[smem,
                  pl.BlockSpec((WBLK, C_W), lambda b, n: (b0 + b * nblk + n, COL_Q // C_W)),
                  blk(kcol, -1), blk(kcol, 0), blk(kcol, 1),
                  blk(vcol, -1), blk(vcol, 0), blk(vcol, 1),
                  cache_spec, cache_spec],
        out_specs=pl.BlockSpec((WBLK, C_W), lambda b, n: (b * nblk + n, 0)),
        out_shape=jax.ShapeDtypeStruct((n_seq * seq_len, C_W), F32),
        compiler_params=_cparams(2),
        name="lat_attention",
    )(sinks, proj, proj, proj, proj, proj, proj, proj, cache_k, cache_v)


def _outproj_kernel(a_ref, bc_ref, bl_ref, cc_ref, cl_ref, x_ref, mod_ref, w_ref, g_ref, wrh_ref, wrl_ref, br_ref,
                    xo_ref, h2_ref, te_ref, tg_ref, rk_ref, cnt_ref, carry, *, n_ctx_tiles):
    i = pl.program_id(0)
    is_ctx = i < n_ctx_tiles
    b_mix = jnp.where(is_ctx, bc_ref[...], bl_ref[...])
    c_mix = jnp.where(is_ctx, cc_ref[...], cl_ref[...])

    @pl.when(i == 0)
    def _():
        carry[...] = jnp.zeros_like(carry)

    mixed = jnp.concatenate([a_ref[...], b_mix, c_mix], axis=-1).astype(BF16)
    mix = jnp.dot(mixed, w_ref[...], preferred_element_type=F32)
    x = x_ref[...] + mod_ref[2:3, :] * mix
    xo_ref[...] = x
    y = x * lax.rsqrt(jnp.mean(x * x, axis=-1, keepdims=True) + EPS) * g_ref[...]
    h2 = y * (1.0 + mod_ref[4:5, :]) + mod_ref[3:4, :]
    h2_ref[...] = h2

    dn = (((1,), (1,)), ((), ()))
    h_hi = h2.astype(BF16)
    h_lo = (h2 - h_hi.astype(F32)).astype(BF16)
    wh = wrh_ref[...]
    logits = (lax.dot_general(wh, h_hi, dn, preferred_element_type=F32)
              + lax.dot_general(wh, h_lo, dn, preferred_element_type=F32)
              + lax.dot_general(wrl_ref[...], h_hi, dn, preferred_element_type=F32)) + br_ref[...]

    eidx = lax.broadcasted_iota(I32, (N_EXP, TM), 0).astype(F32)
    l = logits
    vals, hots = [], []
    for k in range(TOP_K):
        m = jnp.max(l, axis=0, keepdims=True)
        sel = jnp.min(jnp.where(l == m, eidx, float(N_EXP)), axis=0, keepdims=True)
        hot = eidx == sel
        vals.append(m)
        hots.append(hot)
        te_ref[k:k + 1, :] = sel.astype(I32)
        l = jnp.where(hot, -jnp.inf, l)
    exps = [jnp.exp(v - vals[0]) for v in vals]
    den = exps[0] + exps[1] + exps[2] + exps[3]
    for k in range(TOP_K):
        tg_ref[k:k + 1, :] = exps[k] / den

    any_hot = jnp.logical_or(jnp.logical_or(hots[0], hots[1]), jnp.logical_or(hots[2], hots[3]))
    hot_f = any_hot.astype(F32)
    tri = (lax.broadcasted_iota(I32, (TM, TM), 0) < lax.broadcasted_iota(I32, (TM, TM), 1)).astype(BF16)
    pos = jnp.dot(hot_f.astype(BF16), tri, preferred_element_type=F32) + carry[...]
    for k in range(TOP_K):
        rk_ref[k:k + 1, :] = jnp.sum(jnp.where(hots[k], pos, 0.0), axis=0, keepdims=True).astype(I32)
    carry[...] = carry[...] + jnp.sum(hot_f, axis=1, keepdims=True)
    cnt_ref[...] = carry[...].astype(I32)


def _outproj(a_out, b_ctx, b_lat, c_ctx, c_lat, x, mod_l, w_out_bf, g_ffn, w_router, b_router, tile_row,
             n_ctx_tiles):
    t = x.shape[0]

    def ctx_map(i):
        return (jnp.minimum(i, n_ctx_tiles - 1), 0)

    def lat_map(i):
        return (jnp.maximum(i - n_ctx_tiles, 0), 0)

    wr_t = w_router.T
    wr_hi = wr_t.astype(BF16)
    wr_lo = (wr_t - wr_hi.astype(F32)).astype(BF16)
    row4 = pl.BlockSpec((TOP_K, TM), lambda i: (0, i))
    return pl.pallas_call(
        functools.partial(_outproj_kernel, n_ctx_tiles=n_ctx_tiles),
        grid=(t // TM,),
        in_specs=[pl.BlockSpec((TM, A_W), lambda i: (i, 0)),
                  pl.BlockSpec((TM, B_W), ctx_map),
                  pl.BlockSpec((TM, B_W), lat_map),
                  pl.BlockSpec((TM, C_W), ctx_map),
                  pl.BlockSpec((TM, C_W), lat_map),
                  pl.BlockSpec((TM, D_MODEL), lambda i: (i, 0)),
                  pl.BlockSpec((None, 6, D_MODEL), lambda i: (tile_row(i), 0, 0)),
                  pl.BlockSpec((D_MODEL, D_MODEL), lambda i: (0, 0)),
                  pl.BlockSpec((1, D_MODEL), lambda i: (0, 0)),
                  pl.BlockSpec((N_EXP, D_MODEL), lambda i: (0, 0)),
                  pl.BlockSpec((N_EXP, D_MODEL), lambda i: (0, 0)),
                  pl.BlockSpec((N_EXP, 1), lambda i: (0, 0))],
        out_specs=[pl.BlockSpec((TM, D_MODEL), lambda i: (i, 0)),
                   pl.BlockSpec((TM, D_MODEL), lambda i: (i, 0)),
                   row4, row4, row4,
                   pl.BlockSpec((N_EXP, 1), lambda i: (0, 0))],
        out_shape=[jax.ShapeDtypeStruct((t, D_MODEL), F32),
                   jax.ShapeDtypeStruct((t, D_MODEL), F32),
                   jax.ShapeDtypeStruct((TOP_K, t), I32),
                   jax.ShapeDtypeStruct((TOP_K, t), F32),
                   jax.ShapeDtypeStruct((TOP_K, t), I32),
                   jax.ShapeDtypeStruct((N_EXP, 1), I32)],
        scratch_shapes=[pltpu.VMEM((N_EXP, 1), F32)],
        compiler_params=_cparams(1),
        name="outproj_router",
    )(a_out, b_ctx, b_lat, c_ctx, c_lat, x, mod_l, w_out_bf, g_ffn.reshape(1, D_MODEL), wr_hi, wr_lo,
      b_router.reshape(N_EXP, 1))


def _schedule(nn, ntile_e, tile_start, tile_end, nch_tile, n_tiles):
    n_steps = nn * n_tiles
    steps_e = nn * ntile_e
    s_end = jnp.cumsum(steps_e)
    s_start = s_end - steps_e
    total = s_end[-1]
    sidx = jnp.arange(n_steps, dtype=I32)
    valid = sidx < total
    last = jnp.maximum(total - 1, 0)
    sidx_c = jnp.where(valid, sidx, last)
    se = jnp.minimum(jnp.sum(s_end[None, :] <= sidx_c[:, None], axis=1), N_EXP - 1).astype(I32)
    loc = sidx_c - s_start[se]
    nt = jnp.maximum(ntile_e[se], 1)
    sj = loc % nt
    used = tile_end[-1]
    n_unused = jnp.maximum(n_tiles - used, 1)
    extra = jnp.maximum(sidx - total, 0)
    sn = jnp.where(valid, loc // nt, extra // n_unused).astype(I32)
    stile = jnp.where(valid, tile_start[se] + (nt - 1 - sj), used + extra % n_unused)
    stile = jnp.clip(stile, 0, n_tiles - 1).astype(I32)
    snch = jnp.where(valid, nch_tile[stile], 0).astype(I32)
    first = jnp.logical_and(valid, sj == 0).astype(I32)
    return (se, sn, stile, snch, first)


def _route_tables(counts, top_e, rank, t):
    r_tot = t * TOP_K + N_EXP * MOE_TILE
    n_tiles = r_tot // MOE_TILE
    chunks_per_tile = MOE_TILE // MOE_CHK

    ntile_e = (counts + MOE_TILE - 1) // MOE_TILE
    tile_end = jnp.cumsum(ntile_e)
    tile_start = tile_end - ntile_e
    reg_start = tile_start * MOE_TILE

    eids = jnp.arange(N_EXP, dtype=I32)[:, None, None]
    dest = jnp.sum(jnp.where(top_e[None] == eids, reg_start[:, None, None], 0), axis=0) + rank

    tidx = jnp.arange(n_tiles, dtype=I32)
    te = jnp.sum(tile_end[None, :] <= tidx[:, None], axis=1).astype(I32)
    te_c = jnp.minimum(te, N_EXP - 1)
    left = jnp.where(te < N_EXP, counts[te_c] - (tidx - tile_start[te_c]) * MOE_TILE, 0)
    nch_tile = jnp.clip((left + MOE_CHK - 1) // MOE_CHK, 0, chunks_per_tile).astype(I32)

    bidx = jnp.arange(r_tot // MOE_CHK, dtype=I32)
    real_rows = left[bidx // chunks_per_tile] - (bidx % chunks_per_tile) * MOE_CHK
    zflag = (real_rows < MOE_CHK).astype(I32)

    sched_gu = _schedule(D_FF // MOE_TN, ntile_e, tile_start, tile_end, nch_tile, n_tiles)
    sched_dn = _schedule(D_MODEL // MOE_TN_DOWN, ntile_e, tile_start, tile_end, nch_tile, n_tiles)
    return dest, zflag, sched_gu, sched_dn


def _pack_bf16_pairs(h):
    half = h.shape[1] // 2
    lo = lax.bitcast_convert_type(h[:, :half].astype(BF16).astype(F32), I32)
    hi = lax.bitcast_convert_type(h[:, half:].astype(BF16).astype(F32), I32)
    return jnp.bitwise_or(jnp.bitwise_and(hi, HI_MASK), lax.shift_right_logical(lo, 16))


def _unpack_bf16_pairs(p):
    lo = lax.bitcast_convert_type(lax.shift_left(p, 16), F32).astype(BF16)
    hi = lax.bitcast_convert_type(jnp.bitwise_and(p, HI_MASK), F32).astype(BF16)
    return lo, hi


def _scatter_kernel(zflag_ref, dest_ref, h_ref, xs_hbm, buf, zbuf, sem, zsem, *, nblk, nzb):
    i = pl.program_id(0)
    slot = i % 2

    def zero_copy(b):
        r0 = pl.multiple_of(b * MOE_CHK, MOE_CHK)
        return pltpu.make_async_copy(zbuf, xs_hbm.at[pl.ds(r0, MOE_CHK)], zsem)

    @pl.when(i == 0)
    def _():
        zbuf[...] = jnp.zeros_like(zbuf)

        def zstart(b, carry):
            @pl.when(zflag_ref[b] > 0)
            def _():
                zero_copy(b).start()
            return carry

        def zwait(b, carry):
            @pl.when(zflag_ref[b] > 0)
            def _():
                zero_copy(b).wait()
            return carry

        lax.fori_loop(0, nzb, zstart, 0)
        lax.fori_loop(0, nzb, zwait, 0)

    def wait_rows(s):
        def body(r, carry):
            pltpu.make_async_copy(buf.at[s, pl.ds(0, 1)], xs_hbm.at[pl.ds(0, 1)], sem.at[s]).wait()
            return carry
        lax.fori_loop(0, TOP_K * SCAT_ROWS, body, 0, unroll=16)

    @pl.when(i >= 2)
    def _():
        wait_rows(slot)

    buf[slot] = _pack_bf16_pairs(h_ref[...])

    def issue(r, carry):
        for k in range(TOP_K):
            pltpu.make_async_copy(buf.at[slot, pl.ds(r, 1)], xs_hbm.at[pl.ds(dest_ref[k, r], 1)],
                                  sem.at[slot]).start()
        return carry
    lax.fori_loop(0, SCAT_ROWS, issue, 0, unroll=4)

    @pl.when(i == nblk - 1)
    def _():
        wait_rows(slot)
        if nblk > 1:
            wait_rows(1 - slot)


def _scatter_rows(h2, dest, zflag):
    t = h2.shape[0]
    r_tot = zflag.shape[0] * MOE_CHK
    nblk = t // SCAT_ROWS
    idx3 = dest.reshape(TOP_K, nblk, SCAT_ROWS).transpose(1, 0, 2)
    half = D_MODEL // 2
    grid_spec = pltpu.PrefetchScalarGridSpec(
        num_scalar_prefetch=1,
        grid=(nblk,),
        in_specs=[pl.BlockSpec((None, TOP_K, SCAT_ROWS), lambda i, z: (i, 0, 0), memory_space=pltpu.SMEM),
                  pl.BlockSpec((SCAT_ROWS, D_MODEL), lambda i, z: (i, 0))],
        out_specs=pl.BlockSpec(memory_space=pl.ANY),
        scratch_shapes=[pltpu.VMEM((2, SCAT_ROWS, half), I32), pltpu.VMEM((MOE_CHK, half), I32),
                        pltpu.SemaphoreType.DMA((2,)), pltpu.SemaphoreType.DMA(())],
    )
    return pl.pallas_call(
        functools.partial(_scatter_kernel, nblk=nblk, nzb=zflag.shape[0]),
        grid_spec=grid_spec,
        out_shape=jax.ShapeDtypeStruct((r_tot, half), I32),
        compiler_params=_cparams(1),
        name="moe_scatter",
    )(zflag, idx3, h2)


def _gmm_gu_kernel(e_ref, n_ref, t_ref, nch_ref, first_ref, x_ref, wg_ref, wu_ref, bg_ref, bu_ref,
                   o_ref, wg_s, wu_s):
    i = pl.program_id(0)
    del e_ref, n_ref, t_ref

    @pl.when(first_ref[i] > 0)
    def _():
        wg_s[...] = wg_ref[...].astype(BF16)
        wu_s[...] = wu_ref[...].astype(BF16)

    nch = nch_ref[i]

    def body(c, carry):
        r0 = pl.multiple_of(c * MOE_CHK, MOE_CHK)
        half = D_MODEL // 2
        x_lo, x_hi = _unpack_bf16_pairs(x_ref[pl.ds(r0, MOE_CHK), :])
        g = (jnp.dot(x_lo, wg_s[:half, :], preferred_element_type=F32)
             + jnp.dot(x_hi, wg_s[half:, :], preferred_element_type=F32)) + bg_ref[...]
        u = (jnp.dot(x_lo, wu_s[:half, :], preferred_element_type=F32)
             + jnp.dot(x_hi, wu_s[half:, :], preferred_element_type=F32)) + bu_ref[...]
        g = jnp.minimum(g, SWIGLU_LIMIT)
        u = jnp.clip(u, -SWIGLU_LIMIT, SWIGLU_LIMIT)
        act = (u + 1.0) * (g * jax.nn.sigmoid(SWIGLU_ALPHA * g))
        o_ref[pl.ds(r0, MOE_CHK), :] = act.astype(BF16)
        return carry

    lax.fori_loop(0, nch, body, 0)

    def zero(c, carry):
        r0 = pl.multiple_of(c * MOE_CHK, MOE_CHK)
        o_ref[pl.ds(r0, MOE_CHK), :] = jnp.zeros((MOE_CHK, MOE_TN), BF16)
        return carry

    lax.fori_loop(nch, MOE_TILE // MOE_CHK, zero, 0)


def _gmm_down_kernel(e_ref, n_ref, t_ref, nch_ref, first_ref, a_ref, w_ref, b_ref, o_ref, w_s):
    i = pl.program_id(0)
    del e_ref, n_ref, t_ref

    @pl.when(first_ref[i] > 0)
    def _():
        w_s[...] = w_ref[...].astype(BF16)

    nch = nch_ref[i]

    def body(c, carry):
        r0 = pl.multiple_of(c * MOE_CHK, MOE_CHK)
        y = jnp.dot(a_ref[pl.ds(r0, MOE_CHK), :], w_s[...], preferred_element_type=F32) + b_ref[...]
        o_ref[pl.ds(r0, MOE_CHK), :] = y
        return carry

    lax.fori_loop(0, nch, body, 0)

    def zero(c, carry):
        r0 = pl.multiple_of(c * MOE_CHK, MOE_CHK)
        o_ref[pl.ds(r0, MOE_CHK), :] = jnp.zeros((MOE_CHK, MOE_TN_DOWN), F32)
        return carry

    lax.fori_loop(nch, MOE_TILE // MOE_CHK, zero, 0)


def _moe_experts(x_sorted, sched_gu, sched_dn, w_gu, b_gu, w_down, b_down, layer):
    r_tot = x_sorted.shape[0]
    nn = D_FF // MOE_TN
    depth = w_gu.shape[0]
    b_gu4 = b_gu.reshape(depth, N_EXP, 1, 2 * D_FF)
    b_dn4 = b_down.reshape(depth, N_EXP, 1, D_MODEL)

    def wspec(rows, tn, off):
        return pl.BlockSpec((None, None, rows, tn), lambda i, e, n, t, c, f: (layer, e[i], 0, n[i] + off))

    gu_spec = pltpu.PrefetchScalarGridSpec(
        num_scalar_prefetch=5,
        grid=(sched_gu[0].shape[0],),
        in_specs=[pl.BlockSpec((MOE_TILE, D_MODEL // 2), lambda i, e, n, t, c, f: (t[i], 0)),
                  wspec(D_MODEL, MOE_TN, 0), wspec(D_MODEL, MOE_TN, nn), wspec(1, MOE_TN, 0), wspec(1, MOE_TN, nn)],
        out_specs=pl.BlockSpec((MOE_TILE, MOE_TN), lambda i, e, n, t, c, f: (t[i], n[i])),
        scratch_shapes=[pltpu.VMEM((D_MODEL, MOE_TN), BF16), pltpu.VMEM((D_MODEL, MOE_TN), BF16)],
    )
    act = pl.pallas_call(
        _gmm_gu_kernel,
        grid_spec=gu_spec,
        out_shape=jax.ShapeDtypeStruct((r_tot, D_FF), BF16),
        compiler_params=_cparams(1),
        name="moe_gate_up",
    )(*sched_gu, x_sorted, w_gu, w_gu, b_gu4, b_gu4)

    dn_spec = pltpu.PrefetchScalarGridSpec(
        num_scalar_prefetch=5,
        grid=(sched_dn[0].shape[0],),
        in_specs=[pl.BlockSpec((MOE_TILE, D_FF), lambda i, e, n, t, c, f: (t[i], 0)),
                  wspec(D_FF, MOE_TN_DOWN, 0), wspec(1, MOE_TN_DOWN, 0)],
        out_specs=pl.BlockSpec((MOE_TILE, MOE_TN_DOWN), lambda i, e, n, t, c, f: (t[i], n[i])),
        scratch_shapes=[pltpu.VMEM((D_FF, MOE_TN_DOWN), BF16)],
    )
    return pl.pallas_call(
        _gmm_down_kernel,
        grid_spec=dn_spec,
        out_shape=jax.ShapeDtypeStruct((r_tot, D_MODEL), F32),
        compiler_params=_cparams(1),
        name="moe_down",
    )(*sched_dn, act, w_down, b_dn4)


def _combine_kernel(idx_cur, idx_nxt, y_hbm, x_ref, mod_ref, g_ref, o_ref, buf, sem, *, nblk):
    i = pl.program_id(0)

    def row_copy(idx_ref, k, r, slot):
        return pltpu.make_async_copy(y_hbm.at[pl.ds(idx_ref[k, r], 1)], buf.at[slot, k, pl.ds(r, 1)], sem.at[slot])

    def issue(idx_ref, slot):
        def body(r, carry):
            for k in range(TOP_K):
                row_copy(idx_ref, k, r, slot).start()
            return carry
        lax.fori_loop(0, COMB_ROWS, body, 0, unroll=4)

    @pl.when(i == 0)
    def _():
        issue(idx_cur, 0)

    @pl.when(i + 1 < nblk)
    def _():
        issue(idx_nxt, (i + 1) % 2)

    slot = i % 2

    def wbody(r, carry):
        pltpu.make_async_copy(y_hbm.at[pl.ds(0, 1)], buf.at[slot, 0, pl.ds(0, 1)], sem.at[slot]).wait()
        return carry
    lax.fori_loop(0, TOP_K * COMB_ROWS, wbody, 0, unroll=16)

    g = g_ref[...]
    moe = buf[slot, 0] * g[:, 0:1]
    for k in range(1, TOP_K):
        moe = moe + buf[slot, k] * g[:, k:k + 1]
    o_ref[...] = x_ref[...] + mod_ref[5:6, :] * moe


def _combine(y_sorted, dest, gates_t, x_mid, mod_l, blk_row):
    t = x_mid.shape[0]
    nblk = t // COMB_ROWS
    idx3 = dest.reshape(TOP_K, nblk, COMB_ROWS).transpose(1, 0, 2)
    gates = gates_t.T
    return pl.pallas_call(
        functools.partial(_combine_kernel, nblk=nblk),
        grid=(nblk,),
        in_specs=[pl.BlockSpec((None, TOP_K, COMB_ROWS), lambda i: (i, 0, 0), memory_space=pltpu.SMEM),
                  pl.BlockSpec((None, TOP_K, COMB_ROWS), lambda i: (jnp.minimum(i + 1, nblk - 1), 0, 0),
                               memory_space=pltpu.SMEM),
                  pl.BlockSpec(memory_space=pl.ANY),
                  pl.BlockSpec((COMB_ROWS, D_MODEL), lambda i: (i, 0)),
                  pl.BlockSpec((None, 6, D_MODEL), lambda i: (blk_row(i), 0, 0)),
                  pl.BlockSpec((COMB_ROWS, TOP_K), lambda i: (i, 0))],
        out_specs=pl.BlockSpec((COMB_ROWS, D_MODEL), lambda i: (i, 0)),
        out_shape=jax.ShapeDtypeStruct((t, D_MODEL), F32),
        scratch_shapes=[pltpu.VMEM((2, TOP_K, COMB_ROWS, D_MODEL), F32), pltpu.SemaphoreType.DMA((2,))],
        compiler_params=_cparams(1),
        name="moe_combine",
    )(idx3, idx3, y_sorted, x_mid, mod_l, gates)


def _final_norm_kernel(x_ref, g_ref, o_ref):
    x = x_ref[...]
    o_ref[...] = x * lax.rsqrt(jnp.mean(x * x, axis=-1, keepdims=True) + EPS) * g_ref[...]


def _final_norm(x, g, row0, rows):
    blk0 = row0 // TM
    return pl.pallas_call(
        _final_norm_kernel,
        grid=(rows // TM,),
        in_specs=[pl.BlockSpec((TM, D_MODEL), lambda i: (blk0 + i, 0)), pl.BlockSpec((1, D_MODEL), lambda i: (0, 0))],
        out_specs=pl.BlockSpec((TM, D_MODEL), lambda i: (i, 0)),
        out_shape=jax.ShapeDtypeStruct((rows, D_MODEL), F32),
        compiler_params=_cparams(1),
        name="final_norm",
    )(x, g.reshape(1, D_MODEL))


def _rope_tables(n):
    rows = n // GRID_W
    row = jnp.repeat(jnp.arange(rows), GRID_W).astype(F32)
    col = jnp.tile(jnp.arange(GRID_W), rows).astype(F32)
    n_freq = HEAD_DIM // 4
    inv = ROPE_BASE ** (-jnp.arange(n_freq, dtype=F32) / n_freq)
    ang = jnp.stack([row[:, None] * inv, col[:, None] * inv], axis=1)
    cos, sin = jnp.cos(ang), jnp.sin(ang)
    c64 = jnp.stack([cos, cos], axis=2).reshape(n, HEAD_DIM)
    s64 = jnp.stack([-sin, sin], axis=2).reshape(n, HEAD_DIM)
    rep = LANES // HEAD_DIM
    return jnp.tile(c64, (1, rep)), jnp.tile(s64, (1, rep))


def _block_diag(w):
    eye = jnp.eye(B_HEADS, dtype=w.dtype)
    hd = B_W // B_HEADS
    full = w[:, :, :, None, :] * eye[None, :, None, :, None]
    return full.reshape(2, B_W, B_W).astype(BF16)


def kernel(x_prompt, x_sample, cache_k, cache_v, state_lru, c, c_ctx, w_mod, b_mod, g_mix, w_in, w_sp, b_sp, conv_w, conv_b, w_rg, b_rg, w_ig, b_ig, lru_lambda, sinks, w_out, g_ffn, w_router, b_router, w_gu, b_gu, w_down, b_down, g_final):
    nb_c, len_c, _ = x_prompt.shape
    nb_l, len_l, _ = x_sample.shape
    depth = w_mod.shape[0]
    tc, tl = nb_c * len_c, nb_l * len_l
    t = tc + tl
    assert len_c % TM == 0 and len_l % TM == 0 and len_l % GRID_W == 0
    assert len_c % LRU_CHUNK == 0 and len_l % LRU_CHUNK == 0 and tc % len_l == 0

    x = jnp.concatenate([x_prompt.reshape(tc, D_MODEL), x_sample.reshape(tl, D_MODEL)], axis=0)
    n_rows = 1 + nb_l
    r_pad = -(-n_rows // SUBLANES) * SUBLANES
    cc = jnp.zeros((r_pad, D_MODEL), F32).at[0].set(c_ctx).at[1:n_rows].set(c)
    mod = _modulation(cc, w_mod, b_mod).reshape(depth, r_pad, 6, D_MODEL)

    def row_map(rows_per_blk):
        n_ctx_blk = tc // rows_per_blk
        per_seq = len_l // rows_per_blk
        return lambda i: jnp.where(i < n_ctx_blk, 0, 1 + (jnp.maximum(i - n_ctx_blk, 0) // per_seq))

    tile_row = row_map(TM)
    cos_t, sin_t = _rope_tables(len_l)
    w_in_bf = w_in.astype(BF16)
    w_out_bf = w_out.astype(BF16)
    h0_ctx = jnp.zeros((nb_c, 2, B_W), F32)

    ks, vs, ss = [], [], []
    for l in range(depth):
        mod_l = mod[l]
        proj = _inproj(x, mod_l, g_mix[l], w_in_bf[l], cos_t, sin_t, tile_row, tc // TM, len_l // TM)
        a_out = _chunk_mlp(proj, w_sp[l], b_sp[l])
        lru_p = {"conv_w": conv_w[l], "conv_b": conv_b[l], "wr_bd": _block_diag(w_rg[l]), "b_rg": b_rg[l],
                 "wi_bd": _block_diag(w_ig[l]), "b_ig": b_ig[l], "lam": lru_lambda[l]}
        b_ctx, st_ctx = _lru(proj, h0_ctx, lru_p, 0, len_c)
        b_lat, _ = _lru(proj, state_lru[:, l].astype(F32), lru_p, tc, len_l)
        c_ctx_out = _ctx_attention(proj, sinks[l], nb_c, len_c)
        c_lat_out = _lat_attention(proj, sinks[l], cache_k, cache_v, l, tc, nb_l, len_l)
        x_mid, h2, top_e, gates_t, rank, counts = _outproj(
            a_out, b_ctx, b_lat, c_ctx_out, c_lat_out, x, mod_l, w_out_bf[l], g_ffn[l], w_router[l], b_router[l],
            tile_row, tc // TM)
        dest, zflag, sched_gu, sched_dn = _route_tables(counts[:, 0], top_e, rank, t)
        x_sorted = _scatter_rows(h2, dest, zflag)
        y_sorted = _moe_experts(x_sorted, sched_gu, sched_dn, w_gu, b_gu, w_down, b_down, l)
        x = _combine(y_sorted, dest, gates_t, x_mid, mod_l, row_map(COMB_ROWS))

        kctx = proj[:tc, COL_K:COL_K + KV_W].reshape(nb_c, len_c, N_KV, HEAD_DIM).transpose(0, 2, 1, 3)
        vctx = proj[:tc, COL_V:COL_V + KV_W].reshape(nb_c, len_c, N_KV, HEAD_DIM).transpose(0, 2, 1, 3)
        ks.append(kctx)
        vs.append(vctx)
        ss.append(st_ctx)

    y_prompt = _final_norm(x, g_final, 0, tc).reshape(nb_c, len_c, D_MODEL)
    y_sample = _final_norm(x, g_final, tc, tl).reshape(nb_l, len_l, D_MODEL)
    return (y_prompt, y_sample, jnp.stack(ks, axis=1), jnp.stack(vs, axis=1), jnp.stack(ss, axis=1))
```

```python
import functools

import jax
import jax.numpy as jnp
from jax import lax
from jax.experimental import pallas as pl
from jax.experimental.pallas import tpu as pltpu

F32 = jnp.float32
BF16 = jnp.bfloat16
I32 = jnp.int32

D_MODEL = 2048
A_W = 512
A_GROUPS = 4
CHUNK = 128
B_W = 512
B_HEADS = 8
CONV_W = 4
RG_C = 8.0
C_W = 1024
HEAD_DIM = 64
N_HEADS = 16
N_KV = 4
GQA_G = 4
KV_W = 256
WBLK = 128
GRID_W = 64
ROPE_BASE = 10000.0
ATTN_SCALE = HEAD_DIM ** -0.5
N_EXP = 32
TOP_K = 4
D_FF = 2048
SWIGLU_LIMIT = 7.0
SWIGLU_ALPHA = 1.702
EPS = 1e-6
IN_COLS = 2 * A_W + 2 * B_W + C_W + 2 * KV_W
COL_Q = 2 * A_W + 2 * B_W
COL_K = COL_Q + C_W
COL_V = COL_K + KV_W

LANES = 128
SUBLANES = 8
VMEM_LIMIT = 56 * 1024 * 1024
TM = 256
LRU_CHUNK = 256
MOE_TILE = 1024
MOE_CHK = 256
MOE_TN = 512
MOE_TN_DOWN = 1024
SCAT_ROWS = 128
COMB_ROWS = 128
HI_MASK = -65536


def _cparams(n_axes=1):
    return pltpu.CompilerParams(dimension_semantics=("arbitrary",) * n_axes, vmem_limit_bytes=VMEM_LIMIT)


def _mod_kernel(c_ref, w_ref, b_ref, o_ref):
    x = c_ref[...]
    s = (x * jax.nn.sigmoid(x)).astype(BF16)
    o_ref[...] = jnp.dot(s, w_ref[...].astype(BF16), preferred_element_type=F32) + b_ref[...]


def _modulation(cc, w_mod, b_mod):
    depth = w_mod.shape[0]
    r = cc.shape[0]
    tn = 1024
    return pl.pallas_call(
        _mod_kernel,
        grid=(depth, 6 * D_MODEL // tn),
        in_specs=[pl.BlockSpec((r, D_MODEL), lambda l, j: (0, 0)),
                  pl.BlockSpec((None, D_MODEL, tn), lambda l, j: (l, 0, j)),
                  pl.BlockSpec((None, 1, tn), lambda l, j: (l, 0, j))],
        out_specs=pl.BlockSpec((None, r, tn), lambda l, j: (l, 0, j)),
        out_shape=jax.ShapeDtypeStruct((depth, r, 6 * D_MODEL), F32),
        compiler_params=_cparams(2),
        name="modulation",
    )(cc, w_mod, b_mod.reshape(depth, 1, 6 * D_MODEL))


def _swap16(x):
    lane = lax.broadcasted_iota(I32, x.shape, 1)
    first = (lane % 32) < 16
    return jnp.where(first, pltpu.roll(x, LANES - 16, 1), pltpu.roll(x, 16, 1))


def _inproj_kernel(*refs, n_ctx_tiles, n_x):
    mod_ref, g_ref, w_ref, cos_ref, sin_ref, o_ref = refs[n_x:]
    x = _stream_tile(refs[:n_x], n_ctx_tiles)
    y = x * lax.rsqrt(jnp.mean(x * x, axis=-1, keepdims=True) + EPS) * g_ref[...]
    h = y * (1.0 + mod_ref[1:2, :]) + mod_ref[0:1, :]
    o_ref[...] = jnp.dot(h.astype(BF16), w_ref[...], preferred_element_type=F32)

    @pl.when(pl.program_id(0) >= n_ctx_tiles)
    def _():
        cos = cos_ref[...]
        sin = sin_ref[...]
        for j in range((C_W + KV_W) // LANES):
            c0 = COL_Q + j * LANES
            blk = o_ref[:, c0:c0 + LANES]
            o_ref[:, c0:c0 + LANES] = blk * cos + _swap16(blk) * sin


def _stream_specs(stream, n_ctx_tiles):
    if len(stream) == 1:
        return [pl.BlockSpec((TM, D_MODEL), lambda i: (i, 0))]
    return [pl.BlockSpec((TM, D_MODEL), lambda i: (jnp.minimum(i, n_ctx_tiles - 1), 0)),
            pl.BlockSpec((TM, D_MODEL), lambda i: (jnp.maximum(i - n_ctx_tiles, 0), 0))]


def _stream_tile(x_refs, n_ctx_tiles):
    if len(x_refs) == 1:
        return x_refs[0][...]
    return jnp.where(pl.program_id(0) < n_ctx_tiles, x_refs[0][...], x_refs[1][...])


def _inproj(stream, t, mod_l, g_mix, w_in_bf, cos_t, sin_t, tile_row, n_ctx_tiles, tiles_per_seq):
    def pos_map(i):
        return (jnp.maximum(i - n_ctx_tiles, 0) % tiles_per_seq, 0)

    return pl.pallas_call(
        functools.partial(_inproj_kernel, n_ctx_tiles=n_ctx_tiles, n_x=len(stream)),
        grid=(t // TM,),
        in_specs=_stream_specs(stream, n_ctx_tiles) + [
                  pl.BlockSpec((None, 6, D_MODEL), lambda i: (tile_row(i), 0, 0)),
                  pl.BlockSpec((1, D_MODEL), lambda i: (0, 0)),
                  pl.BlockSpec((D_MODEL, IN_COLS), lambda i: (0, 0)),
                  pl.BlockSpec((TM, LANES), pos_map),
                  pl.BlockSpec((TM, LANES), pos_map)],
        out_specs=pl.BlockSpec((TM, IN_COLS), lambda i: (i, 0)),
        out_shape=jax.ShapeDtypeStruct((t, IN_COLS), F32),
        compiler_params=_cparams(1),
        name="inproj",
    )(*stream, mod_l, g_mix.reshape(1, D_MODEL), w_in_bf, cos_t, sin_t)


def _chunk_mlp_kernel(u_ref, v_ref, w_ref, b_ref, o_ref):
    gd = A_W // A_GROUPS
    for g in range(A_GROUPS):
        gu = jax.nn.gelu(u_ref[:, g * gd:(g + 1) * gd])
        gv = jax.nn.gelu(v_ref[:, g * gd:(g + 1) * gd]).astype(BF16)
        s = jnp.dot(w_ref[g].astype(BF16), gv, preferred_element_type=F32) + b_ref[:, g:g + 1]
        o_ref[:, g * gd:(g + 1) * gd] = gu * s


def _chunk_mlp(proj, w_sp, b_sp):
    t = proj.shape[0]
    return pl.pallas_call(
        _chunk_mlp_kernel,
        grid=(t // CHUNK,),
        in_specs=[pl.BlockSpec((CHUNK, A_W), lambda i: (i, 0)),
                  pl.BlockSpec((CHUNK, A_W), lambda i: (i, 1)),
                  pl.BlockSpec((A_GROUPS, CHUNK, CHUNK), lambda i: (0, 0, 0)),
                  pl.BlockSpec((CHUNK, A_GROUPS), lambda i: (0, 0))],
        out_specs=pl.BlockSpec((CHUNK, A_W), lambda i: (i, 0)),
        out_shape=jax.ShapeDtypeStruct((t, A_W), F32),
        compiler_params=_cparams(1),
        name="chunk_mlp",
    )(proj, proj, w_sp, b_sp.T)


def _softplus(z):
    return jnp.maximum(z, 0.0) + jnp.log1p(jnp.exp(-jnp.abs(z)))


def _expm1(x):
    u = jnp.exp(x)
    um1 = u - 1.0
    safe = jnp.logical_and(um1 != 0.0, um1 != -1.0)
    return jnp.where(safe, um1 * x / jnp.log(jnp.where(safe, u, 2.0)), jnp.where(um1 == 0.0, x, -1.0))


def _lru_kernel(xr_ref, gr_ref, h0_ref, cw_ref, cb_ref, wr_ref, br_ref, wi_ref, bi_ref, lam_ref,
                y_ref, st_ref, a_s, b_s, *, seq_len):
    ch = LRU_CHUNK
    nch = seq_len // ch
    ngrp = ch // SUBLANES
    row = lax.broadcasted_iota(I32, (SUBLANES, B_W), 0)

    for d in (0, 1):
        sp = _softplus(-lam_ref[d:d + 1, :])
        cw = cw_ref[d]
        cb = cb_ref[d:d + 1, :]
        br = br_ref[d:d + 1, :]
        bi = bi_ref[d:d + 1, :]

        def chunk_body(ci, h, d=d, sp=sp, cw=cw, cb=cb, br=br, bi=bi):
            c = ci if d == 0 else nch - 1 - ci
            r0 = pl.multiple_of(c * ch, ch)
            xc = xr_ref[pl.ds(r0, ch), :]
            if d == 0:
                hs = pl.multiple_of(jnp.maximum(r0 - SUBLANES, 0), SUBLANES)
                halo = jnp.where(c > 0, xr_ref[pl.ds(hs, SUBLANES), :], 0.0)
                win = jnp.concatenate([halo, xc], axis=0)
                offs = [SUBLANES - (CONV_W - 1) + j for j in range(CONV_W)]
            else:
                hs = pl.multiple_of(jnp.minimum(r0 + ch, seq_len - SUBLANES), SUBLANES)
                halo = jnp.where(c < nch - 1, xr_ref[pl.ds(hs, SUBLANES), :], 0.0)
                win = jnp.concatenate([xc, halo], axis=0)
                offs = [CONV_W - 1 - j for j in range(CONV_W)]
            conv = cb
            for j in range(CONV_W):
                conv = conv + win[offs[j]:offs[j] + ch, :] * cw[j:j + 1, :]
            xb = conv.astype(BF16)
            r = jax.nn.sigmoid(jnp.dot(xb, wr_ref[d], preferred_element_type=F32) + br)
            ig = jax.nn.sigmoid(jnp.dot(xb, wi_ref[d], preferred_element_type=F32) + bi)
            log_a = -RG_C * r * sp
            a_s[...] = jnp.exp(log_a)
            b_s[...] = jnp.sqrt(-_expm1(2.0 * log_a)) * (ig * conv)

            def grp_body(gi, hc):
                g = gi if d == 0 else ngrp - 1 - gi
                q0 = pl.multiple_of(g * SUBLANES, SUBLANES)
                a8 = a_s[pl.ds(q0, SUBLANES), :]
                b8 = b_s[pl.ds(q0, SUBLANES), :]
                for dd in (1, 2, 4):
                    if d == 0:
                        keep = row >= dd
                        sh = dd
                    else:
                        keep = row < SUBLANES - dd
                        sh = SUBLANES - dd
                    a_sh = jnp.where(keep, pltpu.roll(a8, sh, 0), 1.0)
                    b_sh = jnp.where(keep, pltpu.roll(b8, sh, 0), 0.0)
                    b8 = a8 * b_sh + b8
                    a8 = a8 * a_sh
                h8 = a8 * hc + b8
                o0 = pl.multiple_of(r0 + q0, SUBLANES)
                if d == 0:
                    y_ref[pl.ds(o0, SUBLANES), :] = h8
                    return h8[SUBLANES - 1:SUBLANES, :]
                y_ref[pl.ds(o0, SUBLANES), :] = y_ref[pl.ds(o0, SUBLANES), :] + h8
                return h8[0:1, :]

            h = lax.fori_loop(0, ngrp, grp_body, h, unroll=2)
            if d == 1:
                y_ref[pl.ds(r0, ch), :] = jax.nn.gelu(gr_ref[pl.ds(r0, ch), :]) * y_ref[pl.ds(r0, ch), :]
            return h

        h_last = lax.fori_loop(0, nch, chunk_body, h0_ref[d:d + 1, :])
        st_ref[d:d + 1, :] = h_last


def _lru(proj, h0, p, row0, seq_len):
    n_seq = h0.shape[0]
    blk0 = row0 // seq_len
    w2 = pl.BlockSpec((2, B_W), lambda s: (0, 0))
    wbd = pl.BlockSpec((2, B_W, B_W), lambda s: (0, 0, 0))
    y, st = pl.pallas_call(
        functools.partial(_lru_kernel, seq_len=seq_len),
        grid=(n_seq,),
        in_specs=[pl.BlockSpec((seq_len, B_W), lambda s: (blk0 + s, 2)),
                  pl.BlockSpec((seq_len, B_W), lambda s: (blk0 + s, 3)),
                  pl.BlockSpec((None, 2, B_W), lambda s: (s, 0, 0)),
                  pl.BlockSpec((2, CONV_W, B_W), lambda s: (0, 0, 0)),
                  w2, wbd, w2, wbd, w2, w2],
        out_specs=[pl.BlockSpec((seq_len, B_W), lambda s: (s, 0)),
                   pl.BlockSpec((None, 2, B_W), lambda s: (s, 0, 0))],
        out_shape=[jax.ShapeDtypeStruct((n_seq * seq_len, B_W), F32), jax.ShapeDtypeStruct((n_seq, 2, B_W), F32)],
        scratch_shapes=[pltpu.VMEM((LRU_CHUNK, B_W), F32), pltpu.VMEM((LRU_CHUNK, B_W), F32)],
        compiler_params=_cparams(1),
        name=f"rglru_{seq_len}",
    )(proj, proj, h0, p["conv_w"], p["conv_b"], p["wr_bd"], p["b_rg"], p["wi_bd"], p["b_ig"], p["lam"])
    return y, st


def _attend(q_ref, pieces, sinks_ref, o_ref, nq):
    dn = (((1,), (1,)), ((), ()))
    bias = None
    if any(p[2] is not None for p in pieces):
        parts = [jnp.zeros((GQA_G * nq, p[3]), F32) if p[2] is None else jnp.where(p[2], 0.0, -jnp.inf)
                 for p in pieces]
        bias = jnp.concatenate(parts, axis=1)
    for kv in range(N_KV):
        heads = [kv * GQA_G + g for g in range(GQA_G)]
        qg = jnp.concatenate([q_ref[:, h * HEAD_DIM:(h + 1) * HEAD_DIM] for h in heads], axis=0)
        qg = (qg * ATTN_SCALE).astype(BF16)
        sink = jnp.concatenate([jnp.full((nq, 1), sinks_ref[h], F32) for h in heads], axis=0)
        k_all = jnp.concatenate([p[0](kv) for p in pieces], axis=0)
        v_all = jnp.concatenate([p[1](kv) for p in pieces], axis=0)
        s = lax.dot_general(qg, k_all, dn, preferred_element_type=F32)
        if bias is not None:
            s = s + bias
        m = jnp.maximum(sink, jnp.max(s, axis=-1, keepdims=True))
        e = jnp.exp(s - m)
        den = jnp.exp(sink - m) + jnp.sum(e, axis=-1, keepdims=True)
        o = jnp.dot(e.astype(BF16), v_all, preferred_element_type=F32) / den
        for g, h in enumerate(heads):
            o_ref[:, h * HEAD_DIM:(h + 1) * HEAD_DIM] = o[g * nq:(g + 1) * nq, :]


def _ctx_attn_kernel(sinks_ref, q_ref, k_ref, v_ref, o_ref, *, nq):
    def k_fn(kv):
        return k_ref[:, kv * HEAD_DIM:(kv + 1) * HEAD_DIM].astype(BF16)

    def v_fn(kv):
        return v_ref[:, kv * HEAD_DIM:(kv + 1) * HEAD_DIM].astype(BF16)

    _attend(q_ref, [(k_fn, v_fn, None, nq)], sinks_ref, o_ref, nq)


def _ctx_attention(proj, sinks, n_seq, seq_len):
    smem = pl.BlockSpec(memory_space=pltpu.SMEM)
    return pl.pallas_call(
        functools.partial(_ctx_attn_kernel, nq=seq_len),
        grid=(n_seq,),
        in_specs=[smem,
                  pl.BlockSpec((seq_len, C_W), lambda s: (s, COL_Q // C_W)),
                  pl.BlockSpec((seq_len, KV_W), lambda s: (s, COL_K // KV_W)),
                  pl.BlockSpec((seq_len, KV_W), lambda s: (s, COL_V // KV_W))],
        out_specs=pl.BlockSpec((seq_len, C_W), lambda s: (s, 0)),
        out_shape=jax.ShapeDtypeStruct((n_seq * seq_len, C_W), F32),
        compiler_params=_cparams(1),
        name="ctx_attention",
    )(sinks, proj, proj, proj)


def _lat_attn_kernel(sinks_ref, q_ref, kp_ref, kc_ref, kn_ref, vp_ref, vc_ref, vn_ref, ck_ref, cv_ref,
                     o_ref, *, nblk):
    n = pl.program_id(1)
    nq = WBLK
    rows = lax.broadcasted_iota(I32, (GQA_G * nq, WBLK), 0) % nq
    cols = lax.broadcasted_iota(I32, (GQA_G * nq, WBLK), 1)
    mask_prev = jnp.logical_and(cols >= rows, n > 0)
    mask_next = jnp.logical_and(cols <= rows, n < nblk - 1)

    def loc(ref):
        return lambda kv: ref[:, kv * HEAD_DIM:(kv + 1) * HEAD_DIM].astype(BF16)

    def cache(ref):
        return lambda kv: ref[kv].astype(BF16)

    pieces = [(loc(kp_ref), loc(vp_ref), mask_prev, WBLK),
              (loc(kc_ref), loc(vc_ref), None, WBLK),
              (loc(kn_ref), loc(vn_ref), mask_next, WBLK),
              (cache(ck_ref), cache(cv_ref), None, ck_ref.shape[1])]
    _attend(q_ref, pieces, sinks_ref, o_ref, nq)


def _lat_attention(proj, sinks, cache_k, cache_v, layer, row0, n_seq, seq_len):
    nblk = seq_len // WBLK
    b0 = row0 // WBLK
    past = cache_k.shape[3]
    smem = pl.BlockSpec(memory_space=pltpu.SMEM)
    kcol = COL_K // KV_W
    vcol = COL_V // KV_W

    def blk(col, off):
        return pl.BlockSpec((WBLK, KV_W), lambda b, n: (b0 + b * nblk + jnp.clip(n + off, 0, nblk - 1), col))

    cache_spec = pl.BlockSpec((None, None, N_KV, past, HEAD_DIM), lambda b, n: (b, layer, 0, 0, 0))
    return pl.pallas_call(
        functools.partial(_lat_attn_kernel, nblk=nblk),
        grid=(n_seq, nblk),
        in_specs=[smem,
                  pl.BlockSpec((WBLK, C_W), lambda b, n: (b0 + b * nblk + n, COL_Q // C_W)),
                  blk(kcol, -1), blk(kcol, 0), blk(kcol, 1),
                  blk(vcol, -1), blk(vcol, 0), blk(vcol, 1),
                  cache_spec, cache_spec],
        out_specs=pl.BlockSpec((WBLK, C_W), lambda b, n: (b * nblk + n, 0)),
        out_shape=jax.ShapeDtypeStruct((n_seq * seq_len, C_W), F32),
        compiler_params=_cparams(2),
        name="lat_attention",
    )(sinks, proj, proj, proj, proj, proj, proj, proj, cache_k, cache_v)


def _outproj_kernel(*refs, n_ctx_tiles, n_x):
    (a_ref, bc_ref, bl_ref, cc_ref, cl_ref, mod_ref, w_ref, g_ref, wrh_ref, wrl_ref, br_ref,
     xo_ref, h2_ref, te_ref, tg_ref, rk_ref, cnt_ref, carry) = refs[n_x:]
    i = pl.program_id(0)
    is_ctx = i < n_ctx_tiles
    b_mix = jnp.where(is_ctx, bc_ref[...], bl_ref[...])
    c_mix = jnp.where(is_ctx, cc_ref[...], cl_ref[...])
    x_in = _stream_tile(refs[:n_x], n_ctx_tiles)

    @pl.when(i == 0)
    def _():
        carry[...] = jnp.zeros_like(carry)

    mixed = jnp.concatenate([a_ref[...], b_mix, c_mix], axis=-1).astype(BF16)
    mix = jnp.dot(mixed, w_ref[...], preferred_element_type=F32)
    x = x_in + mod_ref[2:3, :] * mix
    xo_ref[...] = x
    y = x * lax.rsqrt(jnp.mean(x * x, axis=-1, keepdims=True) + EPS) * g_ref[...]
    h2 = y * (1.0 + mod_ref[4:5, :]) + mod_ref[3:4, :]
    h2_ref[...] = h2

    dn = (((1,), (1,)), ((), ()))
    h_hi = h2.astype(BF16)
    h_lo = (h2 - h_hi.astype(F32)).astype(BF16)
    wh = wrh_ref[...]
    logits = (lax.dot_general(wh, h_hi, dn, preferred_element_type=F32)
              + lax.dot_general(wh, h_lo, dn, preferred_element_type=F32)
              + lax.dot_general(wrl_ref[...], h_hi, dn, preferred_element_type=F32)) + br_ref[...]

    eidx = lax.broadcasted_iota(I32, (N_EXP, TM), 0).astype(F32)
    l = logits
    vals, hots = [], []
    for k in range(TOP_K):
        m = jnp.max(l, axis=0, keepdims=True)
        sel = jnp.min(jnp.where(l == m, eidx, float(N_EXP)), axis=0, keepdims=True)
        hot = eidx == sel
        vals.append(m)
        hots.append(hot)
        te_ref[k:k + 1, :] = sel.astype(I32)
        l = jnp.where(hot, -jnp.inf, l)
    exps = [jnp.exp(v - vals[0]) for v in vals]
    den = exps[0] + exps[1] + exps[2] + exps[3]
    for k in range(TOP_K):
        tg_ref[k:k + 1, :] = exps[k] / den

    any_hot = jnp.logical_or(jnp.logical_or(hots[0], hots[1]), jnp.logical_or(hots[2], hots[3]))
    hot_f = any_hot.astype(F32)
    tri = (lax.broadcasted_iota(I32, (TM, TM), 0) < lax.broadcasted_iota(I32, (TM, TM), 1)).astype(BF16)
    pos = jnp.dot(hot_f.astype(BF16), tri, preferred_element_type=F32) + carry[...]
    for k in range(TOP_K):
        rk_ref[k:k + 1, :] = jnp.sum(jnp.where(hots[k], pos, 0.0), axis=0, keepdims=True).astype(I32)
    carry[...] = carry[...] + jnp.sum(hot_f, axis=1, keepdims=True)
    cnt_ref[...] = carry[...].astype(I32)


def _outproj(a_out, b_ctx, b_lat, c_ctx, c_lat, stream, mod_l, w_out_bf, g_ffn, w_router, b_router, tile_row,
             n_ctx_tiles):
    t = a_out.shape[0]

    def ctx_map(i):
        return (jnp.minimum(i, n_ctx_tiles - 1), 0)

    def lat_map(i):
        return (jnp.maximum(i - n_ctx_tiles, 0), 0)

    wr_t = w_router.T
    wr_hi = wr_t.astype(BF16)
    wr_lo = (wr_t - wr_hi.astype(F32)).astype(BF16)
    row4 = pl.BlockSpec((TOP_K, TM), lambda i: (0, i))
    return pl.pallas_call(
        functools.partial(_outproj_kernel, n_ctx_tiles=n_ctx_tiles, n_x=len(stream)),
        grid=(t // TM,),
        in_specs=_stream_specs(stream, n_ctx_tiles) + [
                  pl.BlockSpec((TM, A_W), lambda i: (i, 0)),
                  pl.BlockSpec((TM, B_W), ctx_map),
                  pl.BlockSpec((TM, B_W), lat_map),
                  pl.BlockSpec((TM, C_W), ctx_map),
                  pl.BlockSpec((TM, C_W), lat_map),
                  pl.BlockSpec((None, 6, D_MODEL), lambda i: (tile_row(i), 0, 0)),
                  pl.BlockSpec((D_MODEL, D_MODEL), lambda i: (0, 0)),
                  pl.BlockSpec((1, D_MODEL), lambda i: (0, 0)),
                  pl.BlockSpec((N_EXP, D_MODEL), lambda i: (0, 0)),
                  pl.BlockSpec((N_EXP, D_MODEL), lambda i: (0, 0)),
                  pl.BlockSpec((N_EXP, 1), lambda i: (0, 0))],
        out_specs=[pl.BlockSpec((TM, D_MODEL), lambda i: (i, 0)),
                   pl.BlockSpec((TM, D_MODEL), lambda i: (i, 0)),
                   row4, row4, row4,
                   pl.BlockSpec((N_EXP, 1), lambda i: (0, 0))],
        out_shape=[jax.ShapeDtypeStruct((t, D_MODEL), F32),
                   jax.ShapeDtypeStruct((t, D_MODEL), F32),
                   jax.ShapeDtypeStruct((TOP_K, t), I32),
                   jax.ShapeDtypeStruct((TOP_K, t), F32),
                   jax.ShapeDtypeStruct((TOP_K, t), I32),
                   jax.ShapeDtypeStruct((N_EXP, 1), I32)],
        scratch_shapes=[pltpu.VMEM((N_EXP, 1), F32)],
        compiler_params=_cparams(1),
        name="outproj_router",
    )(*stream, a_out, b_ctx, b_lat, c_ctx, c_lat, mod_l, w_out_bf, g_ffn.reshape(1, D_MODEL),
      wr_hi, wr_lo, b_router.reshape(N_EXP, 1))


def _schedule(nn, ntile_e, tile_start, tile_end, nch_tile, n_tiles):
    n_steps = nn * n_tiles
    steps_e = nn * ntile_e
    s_end = jnp.cumsum(steps_e)
    s_start = s_end - steps_e
    total = s_end[-1]
    sidx = jnp.arange(n_steps, dtype=I32)
    valid = sidx < total
    last = jnp.maximum(total - 1, 0)
    sidx_c = jnp.where(valid, sidx, last)
    se = jnp.minimum(jnp.sum(s_end[None, :] <= sidx_c[:, None], axis=1), N_EXP - 1).astype(I32)
    loc = sidx_c - s_start[se]
    nt = jnp.maximum(ntile_e[se], 1)
    sj = loc % nt
    used = tile_end[-1]
    n_unused = jnp.maximum(n_tiles - used, 1)
    extra = jnp.maximum(sidx - total, 0)
    sn = jnp.where(valid, loc // nt, extra // n_unused).astype(I32)
    stile = jnp.where(valid, tile_start[se] + (nt - 1 - sj), used + extra % n_unused)
    stile = jnp.clip(stile, 0, n_tiles - 1).astype(I32)
    snch = jnp.where(valid, nch_tile[stile], 0).astype(I32)
    first = jnp.logical_and(valid, sj == 0).astype(I32)
    return (se, sn, stile, snch, first)


def _route_tables(counts, top_e, rank, t):
    r_tot = t * TOP_K + N_EXP * MOE_TILE
    n_tiles = r_tot // MOE_TILE
    chunks_per_tile = MOE_TILE // MOE_CHK

    ntile_e = (counts + MOE_TILE - 1) // MOE_TILE
    tile_end = jnp.cumsum(ntile_e)
    tile_start = tile_end - ntile_e
    reg_start = tile_start * MOE_TILE

    eids = jnp.arange(N_EXP, dtype=I32)[:, None, None]
    dest = jnp.sum(jnp.where(top_e[None] == eids, reg_start[:, None, None], 0), axis=0) + rank

    tidx = jnp.arange(n_tiles, dtype=I32)
    te = jnp.sum(tile_end[None, :] <= tidx[:, None], axis=1).astype(I32)
    te_c = jnp.minimum(te, N_EXP - 1)
    left = jnp.where(te < N_EXP, counts[te_c] - (tidx - tile_start[te_c]) * MOE_TILE, 0)
    nch_tile = jnp.clip((left + MOE_CHK - 1) // MOE_CHK, 0, chunks_per_tile).astype(I32)

    bidx = jnp.arange(r_tot // MOE_CHK, dtype=I32)
    real_rows = left[bidx // chunks_per_tile] - (bidx % chunks_per_tile) * MOE_CHK
    zflag = (real_rows < MOE_CHK).astype(I32)

    sched_gu = _schedule(D_FF // MOE_TN, ntile_e, tile_start, tile_end, nch_tile, n_tiles)
    sched_dn = _schedule(D_MODEL // MOE_TN_DOWN, ntile_e, tile_start, tile_end, nch_tile, n_tiles)
    return dest, zflag, sched_gu, sched_dn


def _pack_bf16_pairs(h):
    half = h.shape[1] // 2
    lo = lax.bitcast_convert_type(h[:, :half].astype(BF16).astype(F32), I32)
    hi = lax.bitcast_convert_type(h[:, half:].astype(BF16).astype(F32), I32)
    return jnp.bitwise_or(jnp.bitwise_and(hi, HI_MASK), lax.shift_right_logical(lo, 16))


def _unpack_bf16_pairs(p):
    lo = lax.bitcast_convert_type(lax.shift_left(p, 16), F32).astype(BF16)
    hi = lax.bitcast_convert_type(jnp.bitwise_and(p, HI_MASK), F32).astype(BF16)
    return lo, hi


def _scatter_kernel(zflag_ref, dest_ref, h_ref, xs_hbm, buf, zbuf, sem, zsem, *, nblk, nzb):
    i = pl.program_id(0)
    slot = i % 2

    def zero_copy(b):
        r0 = pl.multiple_of(b * MOE_CHK, MOE_CHK)
        return pltpu.make_async_copy(zbuf, xs_hbm.at[pl.ds(r0, MOE_CHK)], zsem)

    @pl.when(i == 0)
    def _():
        zbuf[...] = jnp.zeros_like(zbuf)

        def zstart(b, carry):
            @pl.when(zflag_ref[b] > 0)
            def _():
                zero_copy(b).start()
            return carry

        def zwait(b, carry):
            @pl.when(zflag_ref[b] > 0)
            def _():
                zero_copy(b).wait()
            return carry

        lax.fori_loop(0, nzb, zstart, 0)
        lax.fori_loop(0, nzb, zwait, 0)

    def wait_rows(s):
        def body(r, carry):
            pltpu.make_async_copy(buf.at[s, pl.ds(0, 1)], xs_hbm.at[pl.ds(0, 1)], sem.at[s]).wait()
            return carry
        lax.fori_loop(0, TOP_K * SCAT_ROWS, body, 0, unroll=16)

    @pl.when(i >= 2)
    def _():
        wait_rows(slot)

    packed = _pack_bf16_pairs(h_ref[...])

    for s in (0, 1):
        @pl.when(slot == s)
        def _(s=s):
            buf[s] = packed
            for r in range(SCAT_ROWS):
                for k in range(TOP_K):
                    pltpu.make_async_copy(buf.at[s, pl.ds(r, 1)], xs_hbm.at[pl.ds(dest_ref[k, r], 1)],
                                          sem.at[s]).start()

    @pl.when(i == nblk - 1)
    def _():
        wait_rows(slot)
        if nblk > 1:
            wait_rows(1 - slot)


def _scatter_rows(h2, dest, zflag):
    t = h2.shape[0]
    r_tot = zflag.shape[0] * MOE_CHK
    nblk = t // SCAT_ROWS
    idx3 = dest.reshape(TOP_K, nblk, SCAT_ROWS).transpose(1, 0, 2)
    half = D_MODEL // 2
    grid_spec = pltpu.PrefetchScalarGridSpec(
        num_scalar_prefetch=1,
        grid=(nblk,),
        in_specs=[pl.BlockSpec((None, TOP_K, SCAT_ROWS), lambda i, z: (i, 0, 0), memory_space=pltpu.SMEM),
                  pl.BlockSpec((SCAT_ROWS, D_MODEL), lambda i, z: (i, 0))],
        out_specs=pl.BlockSpec(memory_space=pl.ANY),
        scratch_shapes=[pltpu.VMEM((2, SCAT_ROWS, half), I32), pltpu.VMEM((MOE_CHK, half), I32),
                        pltpu.SemaphoreType.DMA((2,)), pltpu.SemaphoreType.DMA(())],
    )
    return pl.pallas_call(
        functools.partial(_scatter_kernel, nblk=nblk, nzb=zflag.shape[0]),
        grid_spec=grid_spec,
        out_shape=jax.ShapeDtypeStruct((r_tot, half), I32),
        compiler_params=_cparams(1),
        name="moe_scatter",
    )(zflag, idx3, h2)


def _for_valid_chunks(nch, rows_fn):
    big = 2 * MOE_CHK
    npair = nch // 2

    def pair(c, carry):
        rows_fn(pl.multiple_of(c * big, big), big)
        return carry

    lax.fori_loop(0, npair, pair, 0)

    @pl.when(nch % 2 == 1)
    def _():
        rows_fn(pl.multiple_of(npair * big, MOE_CHK), MOE_CHK)


def _gmm_gu_kernel(e_ref, n_ref, t_ref, nch_ref, first_ref, x_ref, wg_ref, wu_ref, bg_ref, bu_ref,
                   o_ref, wg_s, wu_s):
    i = pl.program_id(0)
    del e_ref, n_ref, t_ref

    @pl.when(first_ref[i] > 0)
    def _():
        wg_s[...] = wg_ref[...].astype(BF16)
        wu_s[...] = wu_ref[...].astype(BF16)

    nch = nch_ref[i]

    def rows_fn(r0, rows):
        half = D_MODEL // 2
        x_lo, x_hi = _unpack_bf16_pairs(x_ref[pl.ds(r0, rows), :])
        g = (jnp.dot(x_lo, wg_s[:half, :], preferred_element_type=F32)
             + jnp.dot(x_hi, wg_s[half:, :], preferred_element_type=F32)) + bg_ref[...]
        u = (jnp.dot(x_lo, wu_s[:half, :], preferred_element_type=F32)
             + jnp.dot(x_hi, wu_s[half:, :], preferred_element_type=F32)) + bu_ref[...]
        g = jnp.minimum(g, SWIGLU_LIMIT)
        u = jnp.clip(u, -SWIGLU_LIMIT, SWIGLU_LIMIT)
        act = (u + 1.0) * (g * jax.nn.sigmoid(SWIGLU_ALPHA * g))
        o_ref[pl.ds(r0, rows), :] = act.astype(BF16)

    _for_valid_chunks(nch, rows_fn)

    def zero(c, carry):
        r0 = pl.multiple_of(c * MOE_CHK, MOE_CHK)
        o_ref[pl.ds(r0, MOE_CHK), :] = jnp.zeros((MOE_CHK, MOE_TN), BF16)
        return carry

    lax.fori_loop(nch, MOE_TILE // MOE_CHK, zero, 0)


def _gmm_down_kernel(e_ref, n_ref, t_ref, nch_ref, first_ref, a_ref, w_ref, b_ref, o_ref, w_s):
    i = pl.program_id(0)
    del e_ref, n_ref, t_ref

    @pl.when(first_ref[i] > 0)
    def _():
        w_s[...] = w_ref[...].astype(BF16)

    nch = nch_ref[i]

    def rows_fn(r0, rows):
        y = jnp.dot(a_ref[pl.ds(r0, rows), :], w_s[...], preferred_element_type=F32) + b_ref[...]
        o_ref[pl.ds(r0, rows), :] = y

    _for_valid_chunks(nch, rows_fn)

    def zero(c, carry):
        r0 = pl.multiple_of(c * MOE_CHK, MOE_CHK)
        o_ref[pl.ds(r0, MOE_CHK), :] = jnp.zeros((MOE_CHK, MOE_TN_DOWN), F32)
        return carry

    lax.fori_loop(nch, MOE_TILE // MOE_CHK, zero, 0)


def _moe_experts(x_sorted, sched_gu, sched_dn, w_gu, b_gu, w_down, b_down, layer):
    r_tot = x_sorted.shape[0]
    nn = D_FF // MOE_TN
    depth = w_gu.shape[0]
    b_gu4 = b_gu.reshape(depth, N_EXP, 1, 2 * D_FF)
    b_dn4 = b_down.reshape(depth, N_EXP, 1, D_MODEL)

    def wspec(rows, tn, off):
        return pl.BlockSpec((None, None, rows, tn), lambda i, e, n, t, c, f: (layer, e[i], 0, n[i] + off))

    gu_spec = pltpu.PrefetchScalarGridSpec(
        num_scalar_prefetch=5,
        grid=(sched_gu[0].shape[0],),
        in_specs=[pl.BlockSpec((MOE_TILE, D_MODEL // 2), lambda i, e, n, t, c, f: (t[i], 0)),
                  wspec(D_MODEL, MOE_TN, 0), wspec(D_MODEL, MOE_TN, nn), wspec(1, MOE_TN, 0), wspec(1, MOE_TN, nn)],
        out_specs=pl.BlockSpec((MOE_TILE, MOE_TN), lambda i, e, n, t, c, f: (t[i], n[i])),
        scratch_shapes=[pltpu.VMEM((D_MODEL, MOE_TN), BF16), pltpu.VMEM((D_MODEL, MOE_TN), BF16)],
    )
    act = pl.pallas_call(
        _gmm_gu_kernel,
        grid_spec=gu_spec,
        out_shape=jax.ShapeDtypeStruct((r_tot, D_FF), BF16),
        compiler_params=_cparams(1),
        name="moe_gate_up",
    )(*sched_gu, x_sorted, w_gu, w_gu, b_gu4, b_gu4)

    dn_spec = pltpu.PrefetchScalarGridSpec(
        num_scalar_prefetch=5,
        grid=(sched_dn[0].shape[0],),
        in_specs=[pl.BlockSpec((MOE_TILE, D_FF), lambda i, e, n, t, c, f: (t[i], 0)),
                  wspec(D_FF, MOE_TN_DOWN, 0), wspec(1, MOE_TN_DOWN, 0)],
        out_specs=pl.BlockSpec((MOE_TILE, MOE_TN_DOWN), lambda i, e, n, t, c, f: (t[i], n[i])),
        scratch_shapes=[pltpu.VMEM((D_FF, MOE_TN_DOWN), BF16)],
    )
    return pl.pallas_call(
        _gmm_down_kernel,
        grid_spec=dn_spec,
        out_shape=jax.ShapeDtypeStruct((r_tot, D_MODEL), F32),
        compiler_params=_cparams(1),
        name="moe_down",
    )(*sched_dn, act, w_down, b_dn4)


def _combine_kernel(idx_cur, idx_nxt, y_hbm, x_ref, mod_ref, g_ref, o_ref, buf, sem, *, nblk):
    i = pl.program_id(0)

    def row_copy(idx_ref, k, r, slot):
        return pltpu.make_async_copy(y_hbm.at[pl.ds(idx_ref[k, r], 1)], buf.at[slot, k, pl.ds(r, 1)], sem.at[slot])

    def issue(idx_ref, slot):
        def body(r, carry):
            for k in range(TOP_K):
                row_copy(idx_ref, k, r, slot).start()
            return carry
        lax.fori_loop(0, COMB_ROWS, body, 0, unroll=4)

    @pl.when(i == 0)
    def _():
        issue(idx_cur, 0)

    for s in (0, 1):
        @pl.when(jnp.logical_and(i + 1 < nblk, (i + 1) % 2 == s))
        def _(s=s):
            for r in range(COMB_ROWS):
                for k in range(TOP_K):
                    row_copy(idx_nxt, k, r, s).start()

    slot = i % 2

    def wbody(r, carry):
        pltpu.make_async_copy(y_hbm.at[pl.ds(0, 1)], buf.at[slot, 0, pl.ds(0, 1)], sem.at[slot]).wait()
        return carry
    lax.fori_loop(0, TOP_K * COMB_ROWS, wbody, 0, unroll=16)

    g = g_ref[...]
    moe = buf[slot, 0] * g[:, 0:1]
    for k in range(1, TOP_K):
        moe = moe + buf[slot, k] * g[:, k:k + 1]
    o_ref[...] = x_ref[...] + mod_ref[5:6, :] * moe


def _combine(y_sorted, dest, gates_t, x_mid, mod_l, blk_row):
    t = x_mid.shape[0]
    nblk = t // COMB_ROWS
    idx3 = dest.reshape(TOP_K, nblk, COMB_ROWS).transpose(1, 0, 2)
    gates = gates_t.T
    return pl.pallas_call(
        functools.partial(_combine_kernel, nblk=nblk),
        grid=(nblk,),
        in_specs=[pl.BlockSpec((None, TOP_K, COMB_ROWS), lambda i: (i, 0, 0), memory_space=pltpu.SMEM),
                  pl.BlockSpec((None, TOP_K, COMB_ROWS), lambda i: (jnp.minimum(i + 1, nblk - 1), 0, 0),
                               memory_space=pltpu.SMEM),
                  pl.BlockSpec(memory_space=pl.ANY),
                  pl.BlockSpec((COMB_ROWS, D_MODEL), lambda i: (i, 0)),
                  pl.BlockSpec((None, 6, D_MODEL), lambda i: (blk_row(i), 0, 0)),
                  pl.BlockSpec((COMB_ROWS, TOP_K), lambda i: (i, 0))],
        out_specs=pl.BlockSpec((COMB_ROWS, D_MODEL), lambda i: (i, 0)),
        out_shape=jax.ShapeDtypeStruct((t, D_MODEL), F32),
        scratch_shapes=[pltpu.VMEM((2, TOP_K, COMB_ROWS, D_MODEL), F32), pltpu.SemaphoreType.DMA((2,))],
        compiler_params=_cparams(1),
        name="moe_combine",
    )(idx3, idx3, y_sorted, x_mid, mod_l, gates)


def _final_norm_kernel(x_ref, g_ref, o_ref):
    x = x_ref[...]
    o_ref[...] = x * lax.rsqrt(jnp.mean(x * x, axis=-1, keepdims=True) + EPS) * g_ref[...]


def _final_norm(x, g, row0, rows):
    blk0 = row0 // TM
    return pl.pallas_call(
        _final_norm_kernel,
        grid=(rows // TM,),
        in_specs=[pl.BlockSpec((TM, D_MODEL), lambda i: (blk0 + i, 0)), pl.BlockSpec((1, D_MODEL), lambda i: (0, 0))],
        out_specs=pl.BlockSpec((TM, D_MODEL), lambda i: (i, 0)),
        out_shape=jax.ShapeDtypeStruct((rows, D_MODEL), F32),
        compiler_params=_cparams(1),
        name="final_norm",
    )(x, g.reshape(1, D_MODEL))


def _rope_tables(n):
    rows = n // GRID_W
    row = jnp.repeat(jnp.arange(rows), GRID_W).astype(F32)
    col = jnp.tile(jnp.arange(GRID_W), rows).astype(F32)
    n_freq = HEAD_DIM // 4
    inv = ROPE_BASE ** (-jnp.arange(n_freq, dtype=F32) / n_freq)
    ang = jnp.stack([row[:, None] * inv, col[:, None] * inv], axis=1)
    cos, sin = jnp.cos(ang), jnp.sin(ang)
    c64 = jnp.stack([cos, cos], axis=2).reshape(n, HEAD_DIM)
    s64 = jnp.stack([-sin, sin], axis=2).reshape(n, HEAD_DIM)
    rep = LANES // HEAD_DIM
    return jnp.tile(c64, (1, rep)), jnp.tile(s64, (1, rep))


def _block_diag(w):
    eye = jnp.eye(B_HEADS, dtype=w.dtype)
    hd = B_W // B_HEADS
    full = w[:, :, :, None, :] * eye[None, :, None, :, None]
    return full.reshape(2, B_W, B_W).astype(BF16)


def kernel(x_prompt, x_sample, cache_k, cache_v, state_lru, c, c_ctx, w_mod, b_mod, g_mix, w_in, w_sp, b_sp, conv_w, conv_b, w_rg, b_rg, w_ig, b_ig, lru_lambda, sinks, w_out, g_ffn, w_router, b_router, w_gu, b_gu, w_down, b_down, g_final):
    nb_c, len_c, _ = x_prompt.shape
    nb_l, len_l, _ = x_sample.shape
    depth = w_mod.shape[0]
    tc, tl = nb_c * len_c, nb_l * len_l
    t = tc + tl
    assert len_c % TM == 0 and len_l % TM == 0 and len_l % GRID_W == 0
    assert len_c % LRU_CHUNK == 0 and len_l % LRU_CHUNK == 0 and tc % len_l == 0

    stream = (x_prompt.reshape(tc, D_MODEL), x_sample.reshape(tl, D_MODEL))
    n_rows = 1 + nb_l
    r_pad = -(-n_rows // SUBLANES) * SUBLANES
    cc = jnp.zeros((r_pad, D_MODEL), F32).at[0].set(c_ctx).at[1:n_rows].set(c)
    mod = _modulation(cc, w_mod, b_mod).reshape(depth, r_pad, 6, D_MODEL)

    def row_map(rows_per_blk):
        n_ctx_blk = tc // rows_per_blk
        per_seq = len_l // rows_per_blk
        return lambda i: jnp.where(i < n_ctx_blk, 0, 1 + (jnp.maximum(i - n_ctx_blk, 0) // per_seq))

    tile_row = row_map(TM)
    cos_t, sin_t = _rope_tables(len_l)
    w_in_bf = w_in.astype(BF16)
    w_out_bf = w_out.astype(BF16)
    h0_ctx = jnp.zeros((nb_c, 2, B_W), F32)

    ks, vs, ss = [], [], []
    for l in range(depth):
        mod_l = mod[l]
        proj = _inproj(stream, t, mod_l, g_mix[l], w_in_bf[l], cos_t, sin_t, tile_row, tc // TM, len_l // TM)
        a_out = _chunk_mlp(proj, w_sp[l], b_sp[l])
        lru_p = {"conv_w": conv_w[l], "conv_b": conv_b[l], "wr_bd": _block_diag(w_rg[l]), "b_rg": b_rg[l],
                 "wi_bd": _block_diag(w_ig[l]), "b_ig": b_ig[l], "lam": lru_lambda[l]}
        b_ctx, st_ctx = _lru(proj, h0_ctx, lru_p, 0, len_c)
        b_lat, _ = _lru(proj, state_lru[:, l].astype(F32), lru_p, tc, len_l)
        c_ctx_out = _ctx_attention(proj, sinks[l], nb_c, len_c)
        c_lat_out = _lat_attention(proj, sinks[l], cache_k, cache_v, l, tc, nb_l, len_l)
        x_mid, h2, top_e, gates_t, rank, counts = _outproj(
            a_out, b_ctx, b_lat, c_ctx_out, c_lat_out, stream, mod_l, w_out_bf[l], g_ffn[l], w_router[l],
            b_router[l], tile_row, tc // TM)
        dest, zflag, sched_gu, sched_dn = _route_tables(counts[:, 0], top_e, rank, t)
        x_sorted = _scatter_rows(h2, dest, zflag)
        y_sorted = _moe_experts(x_sorted, sched_gu, sched_dn, w_gu, b_gu, w_down, b_down, l)
        x = _combine(y_sorted, dest, gates_t, x_mid, mod_l, row_map(COMB_ROWS))
        stream = (x,)

        kctx = proj[:tc, COL_K:COL_K + KV_W].reshape(nb_c, len_c, N_KV, HEAD_DIM).transpose(0, 2, 1, 3)
        vctx = proj[:tc, COL_V:COL_V + KV_W].reshape(nb_c, len_c, N_KV, HEAD_DIM).transpose(0, 2, 1, 3)
        ks.append(kctx)
        vs.append(vctx)
        ss.append(st_ctx)

    y_prompt = _final_norm(x, g_final, 0, tc).reshape(nb_c, len_c, D_MODEL)
    y_sample = _final_norm(x, g_final, tc, tl).reshape(nb_l, len_l, D_MODEL)
    return (y_prompt, y_sample, jnp.stack(ks, axis=1), jnp.stack(vs, axis=1), jnp.stack(ss, axis=1))
```

```python
import functools

import jax
import jax.numpy as jnp
from jax import lax
from jax.experimental import pallas as pl
from jax.experimental.pallas import tpu as pltpu

F32 = jnp.float32
BF16 = jnp.bfloat16
I32 = jnp.int32

D_MODEL = 2048
A_W = 512
A_GROUPS = 4
CHUNK = 128
B_W = 512
B_HEADS = 8
CONV_W = 4
RG_C = 8.0
C_W = 1024
HEAD_DIM = 64
N_HEADS = 16
N_KV = 4
GQA_G = 4
KV_W = 256
WBLK = 128
GRID_W = 64
ROPE_BASE = 10000.0
ATTN_SCALE = HEAD_DIM ** -0.5
N_EXP = 32
TOP_K = 4
D_FF = 2048
SWIGLU_LIMIT = 7.0
SWIGLU_ALPHA = 1.702
EPS = 1e-6
IN_COLS = 2 * A_W + 2 * B_W + C_W + 2 * KV_W
COL_Q = 2 * A_W + 2 * B_W
COL_K = COL_Q + C_W
COL_V = COL_K + KV_W

LANES = 128
SUBLANES = 8
VMEM_LIMIT = 56 * 1024 * 1024
TM = 256
LRU_CHUNK = 256
MOE_TILE = 1024
MOE_CHK = 256
MOE_TN = 512
MOE_TN_DOWN = 1024
SCAT_ROWS = 128
COMB_ROWS = 128
HI_MASK = -65536


def _cparams(n_axes=1):
    return pltpu.CompilerParams(dimension_semantics=("arbitrary",) * n_axes, vmem_limit_bytes=VMEM_LIMIT)


def _mod_kernel(c_ref, w_ref, b_ref, o_ref):
    x = c_ref[...]
    s = (x * jax.nn.sigmoid(x)).astype(BF16)
    o_ref[...] = jnp.dot(s, w_ref[...].astype(BF16), preferred_element_type=F32) + b_ref[...]


def _modulation(cc, w_mod, b_mod):
    depth = w_mod.shape[0]
    r = cc.shape[0]
    tn = 1024
    return pl.pallas_call(
        _mod_kernel,
        grid=(depth, 6 * D_MODEL // tn),
        in_specs=[pl.BlockSpec((r, D_MODEL), lambda l, j: (0, 0)),
                  pl.BlockSpec((None, D_MODEL, tn), lambda l, j: (l, 0, j)),
                  pl.BlockSpec((None, 1, tn), lambda l, j: (l, 0, j))],
        out_specs=pl.BlockSpec((None, r, tn), lambda l, j: (l, 0, j)),
        out_shape=jax.ShapeDtypeStruct((depth, r, 6 * D_MODEL), F32),
        compiler_params=_cparams(2),
        name="modulation",
    )(cc, w_mod, b_mod.reshape(depth, 1, 6 * D_MODEL))


def _swap16(x):
    lane = lax.broadcasted_iota(I32, x.shape, 1)
    first = (lane % 32) < 16
    return jnp.where(first, pltpu.roll(x, LANES - 16, 1), pltpu.roll(x, 16, 1))


def _inproj_kernel(*refs, n_ctx_tiles, n_x):
    mod_ref, g_ref, w_ref, cos_ref, sin_ref, o_ref = refs[n_x:]
    x = _stream_tile(refs[:n_x], n_ctx_tiles)
    y = x * lax.rsqrt(jnp.mean(x * x, axis=-1, keepdims=True) + EPS) * g_ref[...]
    h = y * (1.0 + mod_ref[1:2, :]) + mod_ref[0:1, :]
    o_ref[...] = jnp.dot(h.astype(BF16), w_ref[...], preferred_element_type=F32)

    @pl.when(pl.program_id(0) >= n_ctx_tiles)
    def _():
        cos = cos_ref[...]
        sin = sin_ref[...]
        for j in range((C_W + KV_W) // LANES):
            c0 = COL_Q + j * LANES
            blk = o_ref[:, c0:c0 + LANES]
            o_ref[:, c0:c0 + LANES] = blk * cos + _swap16(blk) * sin


def _stream_specs(stream, n_ctx_tiles):
    if len(stream) == 1:
        return [pl.BlockSpec((TM, D_MODEL), lambda i: (i, 0))]
    return [pl.BlockSpec((TM, D_MODEL), lambda i: (jnp.minimum(i, n_ctx_tiles - 1), 0)),
            pl.BlockSpec((TM, D_MODEL), lambda i: (jnp.maximum(i - n_ctx_tiles, 0), 0))]


def _stream_tile(x_refs, n_ctx_tiles):
    if len(x_refs) == 1:
        return x_refs[0][...]
    return jnp.where(pl.program_id(0) < n_ctx_tiles, x_refs[0][...], x_refs[1][...])


def _inproj(stream, t, mod_l, g_mix, w_in_bf, cos_t, sin_t, tile_row, n_ctx_tiles, tiles_per_seq):
    def pos_map(i):
        return (jnp.maximum(i - n_ctx_tiles, 0) % tiles_per_seq, 0)

    return pl.pallas_call(
        functools.partial(_inproj_kernel, n_ctx_tiles=n_ctx_tiles, n_x=len(stream)),
        grid=(t // TM,),
        in_specs=_stream_specs(stream, n_ctx_tiles) + [
                  pl.BlockSpec((None, 6, D_MODEL), lambda i: (tile_row(i), 0, 0)),
                  pl.BlockSpec((1, D_MODEL), lambda i: (0, 0)),
                  pl.BlockSpec((D_MODEL, IN_COLS), lambda i: (0, 0)),
                  pl.BlockSpec((TM, LANES), pos_map),
                  pl.BlockSpec((TM, LANES), pos_map)],
        out_specs=pl.BlockSpec((TM, IN_COLS), lambda i: (i, 0)),
        out_shape=jax.ShapeDtypeStruct((t, IN_COLS), F32),
        compiler_params=_cparams(1),
        name="inproj",
    )(*stream, mod_l, g_mix.reshape(1, D_MODEL), w_in_bf, cos_t, sin_t)


def _chunk_mlp_kernel(u_ref, v_ref, w_ref, b_ref, o_ref):
    gd = A_W // A_GROUPS
    for g in range(A_GROUPS):
        gu = jax.nn.gelu(u_ref[:, g * gd:(g + 1) * gd])
        gv = jax.nn.gelu(v_ref[:, g * gd:(g + 1) * gd]).astype(BF16)
        s = jnp.dot(w_ref[g].astype(BF16), gv, preferred_element_type=F32) + b_ref[:, g:g + 1]
        o_ref[:, g * gd:(g + 1) * gd] = gu * s


def _chunk_mlp(proj, w_sp, b_sp):
    t = proj.shape[0]
    return pl.pallas_call(
        _chunk_mlp_kernel,
        grid=(t // CHUNK,),
        in_specs=[pl.BlockSpec((CHUNK, A_W), lambda i: (i, 0)),
                  pl.BlockSpec((CHUNK, A_W), lambda i: (i, 1)),
                  pl.BlockSpec((A_GROUPS, CHUNK, CHUNK), lambda i: (0, 0, 0)),
                  pl.BlockSpec((CHUNK, A_GROUPS), lambda i: (0, 0))],
        out_specs=pl.BlockSpec((CHUNK, A_W), lambda i: (i, 0)),
        out_shape=jax.ShapeDtypeStruct((t, A_W), F32),
        compiler_params=_cparams(1),
        name="chunk_mlp",
    )(proj, proj, w_sp, b_sp.T)


def _softplus(z):
    return jnp.maximum(z, 0.0) + jnp.log1p(jnp.exp(-jnp.abs(z)))


def _expm1(x):
    u = jnp.exp(x)
    um1 = u - 1.0
    safe = jnp.logical_and(um1 != 0.0, um1 != -1.0)
    return jnp.where(safe, um1 * x / jnp.log(jnp.where(safe, u, 2.0)), jnp.where(um1 == 0.0, x, -1.0))


def _lru_kernel(xr_ref, gr_ref, h0_ref, cw_ref, cb_ref, wr_ref, br_ref, wi_ref, bi_ref, lam_ref,
                y_ref, st_ref, a_s, b_s, *, seq_len):
    ch = LRU_CHUNK
    nch = seq_len // ch
    ngrp = ch // SUBLANES
    row = lax.broadcasted_iota(I32, (SUBLANES, B_W), 0)

    for d in (0, 1):
        sp = _softplus(-lam_ref[d:d + 1, :])
        cw = cw_ref[d]
        cb = cb_ref[d:d + 1, :]
        br = br_ref[d:d + 1, :]
        bi = bi_ref[d:d + 1, :]

        def chunk_body(ci, h, d=d, sp=sp, cw=cw, cb=cb, br=br, bi=bi):
            c = ci if d == 0 else nch - 1 - ci
            r0 = pl.multiple_of(c * ch, ch)
            xc = xr_ref[pl.ds(r0, ch), :]
            if d == 0:
                hs = pl.multiple_of(jnp.maximum(r0 - SUBLANES, 0), SUBLANES)
                halo = jnp.where(c > 0, xr_ref[pl.ds(hs, SUBLANES), :], 0.0)
                win = jnp.concatenate([halo, xc], axis=0)
                offs = [SUBLANES - (CONV_W - 1) + j for j in range(CONV_W)]
            else:
                hs = pl.multiple_of(jnp.minimum(r0 + ch, seq_len - SUBLANES), SUBLANES)
                halo = jnp.where(c < nch - 1, xr_ref[pl.ds(hs, SUBLANES), :], 0.0)
                win = jnp.concatenate([xc, halo], axis=0)
                offs = [CONV_W - 1 - j for j in range(CONV_W)]
            conv = cb
            for j in range(CONV_W):
                conv = conv + win[offs[j]:offs[j] + ch, :] * cw[j:j + 1, :]
            xb = conv.astype(BF16)
            r = jax.nn.sigmoid(jnp.dot(xb, wr_ref[d], preferred_element_type=F32) + br)
            ig = jax.nn.sigmoid(jnp.dot(xb, wi_ref[d], preferred_element_type=F32) + bi)
            log_a = -RG_C * r * sp
            a_s[...] = jnp.exp(log_a)
            b_s[...] = jnp.sqrt(-_expm1(2.0 * log_a)) * (ig * conv)

            def grp_body(gi, hc):
                g = gi if d == 0 else ngrp - 1 - gi
                q0 = pl.multiple_of(g * SUBLANES, SUBLANES)
                a8 = a_s[pl.ds(q0, SUBLANES), :]
                b8 = b_s[pl.ds(q0, SUBLANES), :]
                for dd in (1, 2, 4):
                    if d == 0:
                        keep = row >= dd
                        sh = dd
                    else:
                        keep = row < SUBLANES - dd
                        sh = SUBLANES - dd
                    a_sh = jnp.where(keep, pltpu.roll(a8, sh, 0), 1.0)
                    b_sh = jnp.where(keep, pltpu.roll(b8, sh, 0), 0.0)
                    b8 = a8 * b_sh + b8
                    a8 = a8 * a_sh
                h8 = a8 * hc + b8
                o0 = pl.multiple_of(r0 + q0, SUBLANES)
                if d == 0:
                    y_ref[pl.ds(o0, SUBLANES), :] = h8
                    return h8[SUBLANES - 1:SUBLANES, :]
                y_ref[pl.ds(o0, SUBLANES), :] = y_ref[pl.ds(o0, SUBLANES), :] + h8
                return h8[0:1, :]

            h = lax.fori_loop(0, ngrp, grp_body, h, unroll=2)
            if d == 1:
                y_ref[pl.ds(r0, ch), :] = jax.nn.gelu(gr_ref[pl.ds(r0, ch), :]) * y_ref[pl.ds(r0, ch), :]
            return h

        h_last = lax.fori_loop(0, nch, chunk_body, h0_ref[d:d + 1, :])
        st_ref[d:d + 1, :] = h_last


def _lru(proj, h0, p, row0, seq_len):
    n_seq = h0.shape[0]
    blk0 = row0 // seq_len
    w2 = pl.BlockSpec((2, B_W), lambda s: (0, 0))
    wbd = pl.BlockSpec((2, B_W, B_W), lambda s: (0, 0, 0))
    y, st = pl.pallas_call(
        functools.partial(_lru_kernel, seq_len=seq_len),
        grid=(n_seq,),
        in_specs=[pl.BlockSpec((seq_len, B_W), lambda s: (blk0 + s, 2)),
                  pl.BlockSpec((seq_len, B_W), lambda s: (blk0 + s, 3)),
                  pl.BlockSpec((None, 2, B_W), lambda s: (s, 0, 0)),
                  pl.BlockSpec((2, CONV_W, B_W), lambda s: (0, 0, 0)),
                  w2, wbd, w2, wbd, w2, w2],
        out_specs=[pl.BlockSpec((seq_len, B_W), lambda s: (s, 0)),
                   pl.BlockSpec((None, 2, B_W), lambda s: (s, 0, 0))],
        out_shape=[jax.ShapeDtypeStruct((n_seq * seq_len, B_W), F32), jax.ShapeDtypeStruct((n_seq, 2, B_W), F32)],
        scratch_shapes=[pltpu.VMEM((LRU_CHUNK, B_W), F32), pltpu.VMEM((LRU_CHUNK, B_W), F32)],
        compiler_params=_cparams(1),
        name=f"rglru_{seq_len}",
    )(proj, proj, h0, p["conv_w"], p["conv_b"], p["wr_bd"], p["b_rg"], p["wi_bd"], p["b_ig"], p["lam"])
    return y, st


def _attend(q_ref, pieces, sinks_ref, o_ref, nq):
    dn = (((1,), (1,)), ((), ()))
    bias = None
    if any(p[2] is not None for p in pieces):
        parts = [jnp.zeros((GQA_G * nq, p[3]), F32) if p[2] is None else jnp.where(p[2], 0.0, -jnp.inf)
                 for p in pieces]
        bias = jnp.concatenate(parts, axis=1)
    for kv in range(N_KV):
        heads = [kv * GQA_G + g for g in range(GQA_G)]
        qg = jnp.concatenate([q_ref[:, h * HEAD_DIM:(h + 1) * HEAD_DIM] for h in heads], axis=0)
        qg = (qg * ATTN_SCALE).astype(BF16)
        sink = jnp.concatenate([jnp.full((nq, 1), sinks_ref[h], F32) for h in heads], axis=0)
        k_all = jnp.concatenate([p[0](kv) for p in pieces], axis=0)
        v_all = jnp.concatenate([p[1](kv) for p in pieces], axis=0)
        s = lax.dot_general(qg, k_all, dn, preferred_element_type=F32)
        if bias is not None:
            s = s + bias
        m = jnp.maximum(sink, jnp.max(s, axis=-1, keepdims=True))
        e = jnp.exp(s - m)
        den = jnp.exp(sink - m) + jnp.sum(e, axis=-1, keepdims=True)
        o = jnp.dot(e.astype(BF16), v_all, preferred_element_type=F32) / den
        for g, h in enumerate(heads):
            o_ref[:, h * HEAD_DIM:(h + 1) * HEAD_DIM] = o[g * nq:(g + 1) * nq, :]


def _ctx_attn_kernel(sinks_ref, q_ref, k_ref, v_ref, o_ref, *, nq):
    def k_fn(kv):
        return k_ref[:, kv * HEAD_DIM:(kv + 1) * HEAD_DIM].astype(BF16)

    def v_fn(kv):
        return v_ref[:, kv * HEAD_DIM:(kv + 1) * HEAD_DIM].astype(BF16)

    _attend(q_ref, [(k_fn, v_fn, None, nq)], sinks_ref, o_ref, nq)


def _ctx_attention(proj, sinks, n_seq, seq_len):
    smem = pl.BlockSpec(memory_space=pltpu.SMEM)
    return pl.pallas_call(
        functools.partial(_ctx_attn_kernel, nq=seq_len),
        grid=(n_seq,),
        in_specs=[smem,
                  pl.BlockSpec((seq_len, C_W), lambda s: (s, COL_Q // C_W)),
                  pl.BlockSpec((seq_len, KV_W), lambda s: (s, COL_K // KV_W)),
                  pl.BlockSpec((seq_len, KV_W), lambda s: (s, COL_V // KV_W))],
        out_specs=pl.BlockSpec((seq_len, C_W), lambda s: (s, 0)),
        out_shape=jax.ShapeDtypeStruct((n_seq * seq_len, C_W), F32),
        compiler_params=_cparams(1),
        name="ctx_attention",
    )(sinks, proj, proj, proj)


def _lat_attn_kernel(sinks_ref, q_ref, kp_ref, kc_ref, kn_ref, vp_ref, vc_ref, vn_ref, ck_ref, cv_ref,
                     o_ref, *, nblk):
    n = pl.program_id(1)
    nq = WBLK
    rows = lax.broadcasted_iota(I32, (GQA_G * nq, WBLK), 0) % nq
    cols = lax.broadcasted_iota(I32, (GQA_G * nq, WBLK), 1)
    mask_prev = jnp.logical_and(cols >= rows, n > 0)
    mask_next = jnp.logical_and(cols <= rows, n < nblk - 1)

    def loc(ref):
        return lambda kv: ref[:, kv * HEAD_DIM:(kv + 1) * HEAD_DIM].astype(BF16)

    def cache(ref):
        return lambda kv: ref[kv].astype(BF16)

    pieces = [(loc(kp_ref), loc(vp_ref), mask_prev, WBLK),
              (loc(kc_ref), loc(vc_ref), None, WBLK),
              (loc(kn_ref), loc(vn_ref), mask_next, WBLK),
              (cache(ck_ref), cache(cv_ref), None, ck_ref.shape[1])]
    _attend(q_ref, pieces, sinks_ref, o_ref, nq)


def _lat_attention(proj, sinks, cache_k, cache_v, layer, row0, n_seq, seq_len):
    nblk = seq_len // WBLK
    b0 = row0 // WBLK
    past = cache_k.shape[3]
    smem = pl.BlockSpec(memory_space=pltpu.SMEM)
    kcol = COL_K // KV_W
    vcol = COL_V // KV_W

    def blk(col, off):
        return pl.BlockSpec((WBLK, KV_W), lambda b, n: (b0 + b * nblk + jnp.clip(n + off, 0, nblk - 1), col))

    cache_spec = pl.BlockSpec((None, None, N_KV, past, HEAD_DIM), lambda b, n: (b, layer, 0, 0, 0))
    return pl.pallas_call(
        functools.partial(_lat_attn_kernel, nblk=nblk),
        grid=(n_seq, nblk),
        in_specs=[smem,
                  pl.BlockSpec((WBLK, C_W), lambda b, n: (b0 + b * nblk + n, COL_Q // C_W)),
                  blk(kcol, -1), blk(kcol, 0), blk(kcol, 1),
                  blk(vcol, -1), blk(vcol, 0), blk(vcol, 1),
                  cache_spec, cache_spec],
        out_specs=pl.BlockSpec((WBLK, C_W), lambda b, n: (b * nblk + n, 0)),
        out_shape=jax.ShapeDtypeStruct((n_seq * seq_len, C_W), F32),
        compiler_params=_cparams(2),
        name="lat_attention",
    )(sinks, proj, proj, proj, proj, proj, proj, proj, cache_k, cache_v)


def _outproj_kernel(*refs, n_ctx_tiles, n_x):
    (a_ref, bc_ref, bl_ref, cc_ref, cl_ref, mod_ref, w_ref, g_ref, wrh_ref, wrl_ref, br_ref,
     xo_ref, h2_ref, te_ref, tg_ref, rk_ref, cnt_ref, carry) = refs[n_x:]
    i = pl.program_id(0)
    is_ctx = i < n_ctx_tiles
    b_mix = jnp.where(is_ctx, bc_ref[...], bl_ref[...])
    c_mix = jnp.where(is_ctx, cc_ref[...], cl_ref[...])
    x_in = _stream_tile(refs[:n_x], n_ctx_tiles)

    @pl.when(i == 0)
    def _():
        carry[...] = jnp.zeros_like(carry)

    mixed = jnp.concatenate([a_ref[...], b_mix, c_mix], axis=-1).astype(BF16)
    mix = jnp.dot(mixed, w_ref[...], preferred_element_type=F32)
    x = x_in + mod_ref[2:3, :] * mix
    xo_ref[...] = x
    y = x * lax.rsqrt(jnp.mean(x * x, axis=-1, keepdims=True) + EPS) * g_ref[...]
    h2 = y * (1.0 + mod_ref[4:5, :]) + mod_ref[3:4, :]
    h2_ref[...] = _pack_bf16_pairs(h2)

    dn = (((1,), (1,)), ((), ()))
    h_hi = h2.astype(BF16)
    h_lo = (h2 - h_hi.astype(F32)).astype(BF16)
    wh = wrh_ref[...]
    logits = (lax.dot_general(wh, h_hi, dn, preferred_element_type=F32)
              + lax.dot_general(wh, h_lo, dn, preferred_element_type=F32)
              + lax.dot_general(wrl_ref[...], h_hi, dn, preferred_element_type=F32)) + br_ref[...]

    eidx = lax.broadcasted_iota(I32, (N_EXP, TM), 0).astype(F32)
    l = logits
    vals, hots = [], []
    for k in range(TOP_K):
        m = jnp.max(l, axis=0, keepdims=True)
        sel = jnp.min(jnp.where(l == m, eidx, float(N_EXP)), axis=0, keepdims=True)
        hot = eidx == sel
        vals.append(m)
        hots.append(hot)
        te_ref[k:k + 1, :] = sel.astype(I32)
        l = jnp.where(hot, -jnp.inf, l)
    exps = [jnp.exp(v - vals[0]) for v in vals]
    den = exps[0] + exps[1] + exps[2] + exps[3]
    for k in range(TOP_K):
        tg_ref[k:k + 1, :] = exps[k] / den

    any_hot = jnp.logical_or(jnp.logical_or(hots[0], hots[1]), jnp.logical_or(hots[2], hots[3]))
    hot_f = any_hot.astype(F32)
    tri = (lax.broadcasted_iota(I32, (TM, TM), 0) < lax.broadcasted_iota(I32, (TM, TM), 1)).astype(BF16)
    pos = jnp.dot(hot_f.astype(BF16), tri, preferred_element_type=F32) + carry[...]
    for k in range(TOP_K):
        rk_ref[k:k + 1, :] = jnp.sum(jnp.where(hots[k], pos, 0.0), axis=0, keepdims=True).astype(I32)
    carry[...] = carry[...] + jnp.sum(hot_f, axis=1, keepdims=True)
    cnt_ref[...] = carry[...].astype(I32)


def _outproj(a_out, b_ctx, b_lat, c_ctx, c_lat, stream, mod_l, w_out_bf, g_ffn, w_router, b_router, tile_row,
             n_ctx_tiles):
    t = a_out.shape[0]

    def ctx_map(i):
        return (jnp.minimum(i, n_ctx_tiles - 1), 0)

    def lat_map(i):
        return (jnp.maximum(i - n_ctx_tiles, 0), 0)

    wr_t = w_router.T
    wr_hi = wr_t.astype(BF16)
    wr_lo = (wr_t - wr_hi.astype(F32)).astype(BF16)
    row4 = pl.BlockSpec((TOP_K, TM), lambda i: (0, i))
    return pl.pallas_call(
        functools.partial(_outproj_kernel, n_ctx_tiles=n_ctx_tiles, n_x=len(stream)),
        grid=(t // TM,),
        in_specs=_stream_specs(stream, n_ctx_tiles) + [
                  pl.BlockSpec((TM, A_W), lambda i: (i, 0)),
                  pl.BlockSpec((TM, B_W), ctx_map),
                  pl.BlockSpec((TM, B_W), lat_map),
                  pl.BlockSpec((TM, C_W), ctx_map),
                  pl.BlockSpec((TM, C_W), lat_map),
                  pl.BlockSpec((None, 6, D_MODEL), lambda i: (tile_row(i), 0, 0)),
                  pl.BlockSpec((D_MODEL, D_MODEL), lambda i: (0, 0)),
                  pl.BlockSpec((1, D_MODEL), lambda i: (0, 0)),
                  pl.BlockSpec((N_EXP, D_MODEL), lambda i: (0, 0)),
                  pl.BlockSpec((N_EXP, D_MODEL), lambda i: (0, 0)),
                  pl.BlockSpec((N_EXP, 1), lambda i: (0, 0))],
        out_specs=[pl.BlockSpec((TM, D_MODEL), lambda i: (i, 0)),
                   pl.BlockSpec((TM, D_MODEL // 2), lambda i: (i, 0)),
                   row4, row4, row4,
                   pl.BlockSpec((N_EXP, 1), lambda i: (0, 0))],
        out_shape=[jax.ShapeDtypeStruct((t, D_MODEL), F32),
                   jax.ShapeDtypeStruct((t, D_MODEL // 2), I32),
                   jax.ShapeDtypeStruct((TOP_K, t), I32),
                   jax.ShapeDtypeStruct((TOP_K, t), F32),
                   jax.ShapeDtypeStruct((TOP_K, t), I32),
                   jax.ShapeDtypeStruct((N_EXP, 1), I32)],
        scratch_shapes=[pltpu.VMEM((N_EXP, 1), F32)],
        compiler_params=_cparams(1),
        name="outproj_router",
    )(*stream, a_out, b_ctx, b_lat, c_ctx, c_lat, mod_l, w_out_bf, g_ffn.reshape(1, D_MODEL),
      wr_hi, wr_lo, b_router.reshape(N_EXP, 1))


def _schedule(nn, ntile_e, tile_start, tile_end, nch_tile, n_tiles):
    n_steps = nn * n_tiles
    steps_e = nn * ntile_e
    s_end = jnp.cumsum(steps_e)
    s_start = s_end - steps_e
    total = s_end[-1]
    sidx = jnp.arange(n_steps, dtype=I32)
    valid = sidx < total
    last = jnp.maximum(total - 1, 0)
    sidx_c = jnp.where(valid, sidx, last)
    se = jnp.minimum(jnp.sum(s_end[None, :] <= sidx_c[:, None], axis=1), N_EXP - 1).astype(I32)
    loc = sidx_c - s_start[se]
    nt = jnp.maximum(ntile_e[se], 1)
    sj = loc % nt
    used = tile_end[-1]
    n_unused = jnp.maximum(n_tiles - used, 1)
    extra = jnp.maximum(sidx - total, 0)
    sn = jnp.where(valid, loc // nt, extra // n_unused).astype(I32)
    stile = jnp.where(valid, tile_start[se] + (nt - 1 - sj), used + extra % n_unused)
    stile = jnp.clip(stile, 0, n_tiles - 1).astype(I32)
    snch = jnp.where(valid, nch_tile[stile], 0).astype(I32)
    first = jnp.logical_and(valid, sj == 0).astype(I32)
    return (se, sn, stile, snch, first)


def _route_tables(counts, top_e, rank, t):
    r_tot = t * TOP_K + N_EXP * MOE_TILE
    n_tiles = r_tot // MOE_TILE
    chunks_per_tile = MOE_TILE // MOE_CHK

    ntile_e = (counts + MOE_TILE - 1) // MOE_TILE
    tile_end = jnp.cumsum(ntile_e)
    tile_start = tile_end - ntile_e
    reg_start = tile_start * MOE_TILE

    eids = jnp.arange(N_EXP, dtype=I32)[:, None, None]
    dest = jnp.sum(jnp.where(top_e[None] == eids, reg_start[:, None, None], 0), axis=0) + rank

    tidx = jnp.arange(n_tiles, dtype=I32)
    te = jnp.sum(tile_end[None, :] <= tidx[:, None], axis=1).astype(I32)
    te_c = jnp.minimum(te, N_EXP - 1)
    left = jnp.where(te < N_EXP, counts[te_c] - (tidx - tile_start[te_c]) * MOE_TILE, 0)
    nch_tile = jnp.clip((left + MOE_CHK - 1) // MOE_CHK, 0, chunks_per_tile).astype(I32)

    bidx = jnp.arange(r_tot // MOE_CHK, dtype=I32)
    real_rows = left[bidx // chunks_per_tile] - (bidx % chunks_per_tile) * MOE_CHK
    zflag = (real_rows < MOE_CHK).astype(I32)

    sched_gu = _schedule(D_FF // MOE_TN, ntile_e, tile_start, tile_end, nch_tile, n_tiles)
    sched_dn = _schedule(D_MODEL // MOE_TN_DOWN, ntile_e, tile_start, tile_end, nch_tile, n_tiles)
    return dest, zflag, sched_gu, sched_dn


def _pack_bf16_pairs(h):
    half = h.shape[1] // 2
    lo = lax.bitcast_convert_type(h[:, :half].astype(BF16).astype(F32), I32)
    hi = lax.bitcast_convert_type(h[:, half:].astype(BF16).astype(F32), I32)
    return jnp.bitwise_or(jnp.bitwise_and(hi, HI_MASK), lax.shift_right_logical(lo, 16))


def _unpack_bf16_pairs(p):
    lo = lax.bitcast_convert_type(lax.shift_left(p, 16), F32).astype(BF16)
    hi = lax.bitcast_convert_type(jnp.bitwise_and(p, HI_MASK), F32).astype(BF16)
    return lo, hi


def _scatter_kernel(zflag_ref, dest_ref, h_ref, xs_hbm, buf, zbuf, sem, zsem, *, nblk, nzb):
    i = pl.program_id(0)
    slot = i % 2

    def zero_copy(b):
        r0 = pl.multiple_of(b * MOE_CHK, MOE_CHK)
        return pltpu.make_async_copy(zbuf, xs_hbm.at[pl.ds(r0, MOE_CHK)], zsem)

    @pl.when(i == 0)
    def _():
        zbuf[...] = jnp.zeros_like(zbuf)

        def zstart(b, carry):
            @pl.when(zflag_ref[b] > 0)
            def _():
                zero_copy(b).start()
            return carry

        def zwait(b, carry):
            @pl.when(zflag_ref[b] > 0)
            def _():
                zero_copy(b).wait()
            return carry

        lax.fori_loop(0, nzb, zstart, 0)
        lax.fori_loop(0, nzb, zwait, 0)

    def wait_rows(s):
        def body(r, carry):
            pltpu.make_async_copy(buf.at[s, pl.ds(0, 1)], xs_hbm.at[pl.ds(0, 1)], sem.at[s]).wait()
            return carry
        lax.fori_loop(0, TOP_K * SCAT_ROWS, body, 0, unroll=16)

    @pl.when(i >= 2)
    def _():
        wait_rows(slot)

    packed = h_ref[...]

    for s in (0, 1):
        @pl.when(slot == s)
        def _(s=s):
            buf[s] = packed
            for r in range(SCAT_ROWS):
                for k in range(TOP_K):
                    pltpu.make_async_copy(buf.at[s, pl.ds(r, 1)], xs_hbm.at[pl.ds(dest_ref[k, r], 1)],
                                          sem.at[s]).start()

    @pl.when(i == nblk - 1)
    def _():
        wait_rows(slot)
        if nblk > 1:
            wait_rows(1 - slot)


def _scatter_rows(h2, dest, zflag):
    t = h2.shape[0]
    r_tot = zflag.shape[0] * MOE_CHK
    nblk = t // SCAT_ROWS
    idx3 = dest.reshape(TOP_K, nblk, SCAT_ROWS).transpose(1, 0, 2)
    half = D_MODEL // 2
    grid_spec = pltpu.PrefetchScalarGridSpec(
        num_scalar_prefetch=1,
        grid=(nblk,),
        in_specs=[pl.BlockSpec((None, TOP_K, SCAT_ROWS), lambda i, z: (i, 0, 0), memory_space=pltpu.SMEM),
                  pl.BlockSpec((SCAT_ROWS, D_MODEL // 2), lambda i, z: (i, 0))],
        out_specs=pl.BlockSpec(memory_space=pl.ANY),
        scratch_shapes=[pltpu.VMEM((2, SCAT_ROWS, half), I32), pltpu.VMEM((MOE_CHK, half), I32),
                        pltpu.SemaphoreType.DMA((2,)), pltpu.SemaphoreType.DMA(())],
    )
    return pl.pallas_call(
        functools.partial(_scatter_kernel, nblk=nblk, nzb=zflag.shape[0]),
        grid_spec=grid_spec,
        out_shape=jax.ShapeDtypeStruct((r_tot, half), I32),
        compiler_params=_cparams(1),
        name="moe_scatter",
    )(zflag, idx3, h2)


def _for_valid_chunks(nch, rows_fn):
    full = MOE_TILE // MOE_CHK
    big = 2 * MOE_CHK

    @pl.when(nch == full)
    def _():
        rows_fn(0, MOE_TILE)

    @pl.when(nch != full)
    def _():
        npair = nch // 2

        def pair(c, carry):
            rows_fn(pl.multiple_of(c * big, big), big)
            return carry

        lax.fori_loop(0, npair, pair, 0)

        @pl.when(nch % 2 == 1)
        def _():
            rows_fn(pl.multiple_of(npair * big, MOE_CHK), MOE_CHK)


def _gmm_gu_kernel(e_ref, n_ref, t_ref, nch_ref, first_ref, x_ref, wg_ref, wu_ref, bg_ref, bu_ref,
                   o_ref, wg_s, wu_s):
    i = pl.program_id(0)
    del e_ref, n_ref, t_ref

    @pl.when(first_ref[i] > 0)
    def _():
        wg_s[...] = wg_ref[...].astype(BF16)
        wu_s[...] = wu_ref[...].astype(BF16)

    nch = nch_ref[i]

    def rows_fn(r0, rows):
        half = D_MODEL // 2
        x_lo, x_hi = _unpack_bf16_pairs(x_ref[pl.ds(r0, rows), :])
        g = (jnp.dot(x_lo, wg_s[:half, :], preferred_element_type=F32)
             + jnp.dot(x_hi, wg_s[half:, :], preferred_element_type=F32)) + bg_ref[...]
        u = (jnp.dot(x_lo, wu_s[:half, :], preferred_element_type=F32)
             + jnp.dot(x_hi, wu_s[half:, :], preferred_element_type=F32)) + bu_ref[...]
        g = jnp.minimum(g, SWIGLU_LIMIT)
        u = jnp.clip(u, -SWIGLU_LIMIT, SWIGLU_LIMIT)
        act = (u + 1.0) * (g * jax.nn.sigmoid(SWIGLU_ALPHA * g))
        o_ref[pl.ds(r0, rows), :] = act.astype(BF16)

    _for_valid_chunks(nch, rows_fn)

    def zero(c, carry):
        r0 = pl.multiple_of(c * MOE_CHK, MOE_CHK)
        o_ref[pl.ds(r0, MOE_CHK), :] = jnp.zeros((MOE_CHK, MOE_TN), BF16)
        return carry

    lax.fori_loop(nch, MOE_TILE // MOE_CHK, zero, 0)


def _gmm_down_kernel(e_ref, n_ref, t_ref, nch_ref, first_ref, a_ref, w_ref, b_ref, o_ref, w_s):
    i = pl.program_id(0)
    del e_ref, n_ref, t_ref

    @pl.when(first_ref[i] > 0)
    def _():
        w_s[...] = w_ref[...].astype(BF16)

    nch = nch_ref[i]

    def rows_fn(r0, rows):
        y = jnp.dot(a_ref[pl.ds(r0, rows), :], w_s[...], preferred_element_type=F32) + b_ref[...]
        o_ref[pl.ds(r0, rows), :] = y

    _for_valid_chunks(nch, rows_fn)

    def zero(c, carry):
        r0 = pl.multiple_of(c * MOE_CHK, MOE_CHK)
        o_ref[pl.ds(r0, MOE_CHK), :] = jnp.zeros((MOE_CHK, MOE_TN_DOWN), F32)
        return carry

    lax.fori_loop(nch, MOE_TILE // MOE_CHK, zero, 0)


def _moe_experts(x_sorted, sched_gu, sched_dn, w_gu, b_gu, w_down, b_down, layer):
    r_tot = x_sorted.shape[0]
    nn = D_FF // MOE_TN
    depth = w_gu.shape[0]
    b_gu4 = b_gu.reshape(depth, N_EXP, 1, 2 * D_FF)
    b_dn4 = b_down.reshape(depth, N_EXP, 1, D_MODEL)

    def wspec(rows, tn, off):
        return pl.BlockSpec((None, None, rows, tn), lambda i, e, n, t, c, f: (layer, e[i], 0, n[i] + off))

    gu_spec = pltpu.PrefetchScalarGridSpec(
        num_scalar_prefetch=5,
        grid=(sched_gu[0].shape[0],),
        in_specs=[pl.BlockSpec((MOE_TILE, D_MODEL // 2), lambda i, e, n, t, c, f: (t[i], 0)),
                  wspec(D_MODEL, MOE_TN, 0), wspec(D_MODEL, MOE_TN, nn), wspec(1, MOE_TN, 0), wspec(1, MOE_TN, nn)],
        out_specs=pl.BlockSpec((MOE_TILE, MOE_TN), lambda i, e, n, t, c, f: (t[i], n[i])),
        scratch_shapes=[pltpu.VMEM((D_MODEL, MOE_TN), BF16), pltpu.VMEM((D_MODEL, MOE_TN), BF16)],
    )
    act = pl.pallas_call(
        _gmm_gu_kernel,
        grid_spec=gu_spec,
        out_shape=jax.ShapeDtypeStruct((r_tot, D_FF), BF16),
        compiler_params=_cparams(1),
        name="moe_gate_up",
    )(*sched_gu, x_sorted, w_gu, w_gu, b_gu4, b_gu4)

    dn_spec = pltpu.PrefetchScalarGridSpec(
        num_scalar_prefetch=5,
        grid=(sched_dn[0].shape[0],),
        in_specs=[pl.BlockSpec((MOE_TILE, D_FF), lambda i, e, n, t, c, f: (t[i], 0)),
                  wspec(D_FF, MOE_TN_DOWN, 0), wspec(1, MOE_TN_DOWN, 0)],
        out_specs=pl.BlockSpec((MOE_TILE, MOE_TN_DOWN), lambda i, e, n, t, c, f: (t[i], n[i])),
        scratch_shapes=[pltpu.VMEM((D_FF, MOE_TN_DOWN), BF16)],
    )
    return pl.pallas_call(
        _gmm_down_kernel,
        grid_spec=dn_spec,
        out_shape=jax.ShapeDtypeStruct((r_tot, D_MODEL), F32),
        compiler_params=_cparams(1),
        name="moe_down",
    )(*sched_dn, act, w_down, b_dn4)


def _combine_kernel(idx_cur, idx_nxt, y_hbm, x_ref, mod_ref, g_ref, *rest, nblk, n_ctx_blk):
    if n_ctx_blk is None:
        o_ref, buf, sem = rest
    else:
        gf_ref, oc_ref, ol_ref, buf, sem = rest
    i = pl.program_id(0)

    def row_copy(idx_ref, k, r, slot):
        return pltpu.make_async_copy(y_hbm.at[pl.ds(idx_ref[k, r], 1)], buf.at[slot, k, pl.ds(r, 1)], sem.at[slot])

    def issue(idx_ref, slot):
        def body(r, carry):
            for k in range(TOP_K):
                row_copy(idx_ref, k, r, slot).start()
            return carry
        lax.fori_loop(0, COMB_ROWS, body, 0, unroll=4)

    @pl.when(i == 0)
    def _():
        issue(idx_cur, 0)

    for s in (0, 1):
        @pl.when(jnp.logical_and(i + 1 < nblk, (i + 1) % 2 == s))
        def _(s=s):
            for r in range(COMB_ROWS):
                for k in range(TOP_K):
                    row_copy(idx_nxt, k, r, s).start()

    slot = i % 2

    def wbody(r, carry):
        pltpu.make_async_copy(y_hbm.at[pl.ds(0, 1)], buf.at[slot, 0, pl.ds(0, 1)], sem.at[slot]).wait()
        return carry
    lax.fori_loop(0, TOP_K * COMB_ROWS, wbody, 0, unroll=16)

    g = g_ref[...]
    moe = buf[slot, 0] * g[:, 0:1]
    for k in range(1, TOP_K):
        moe = moe + buf[slot, k] * g[:, k:k + 1]
    x = x_ref[...] + mod_ref[5:6, :] * moe
    if n_ctx_blk is None:
        o_ref[...] = x
    else:
        y = x * lax.rsqrt(jnp.mean(x * x, axis=-1, keepdims=True) + EPS) * gf_ref[...]

        @pl.when(i < n_ctx_blk)
        def _():
            oc_ref[...] = y

        @pl.when(i >= n_ctx_blk)
        def _():
            ol_ref[...] = y


def _combine(y_sorted, dest, gates_t, x_mid, mod_l, blk_row, final=None):
    t = x_mid.shape[0]
    nblk = t // COMB_ROWS
    idx3 = dest.reshape(TOP_K, nblk, COMB_ROWS).transpose(1, 0, 2)
    gates = gates_t.T
    in_specs = [pl.BlockSpec((None, TOP_K, COMB_ROWS), lambda i: (i, 0, 0), memory_space=pltpu.SMEM),
                pl.BlockSpec((None, TOP_K, COMB_ROWS), lambda i: (jnp.minimum(i + 1, nblk - 1), 0, 0),
                             memory_space=pltpu.SMEM),
                pl.BlockSpec(memory_space=pl.ANY),
                pl.BlockSpec((COMB_ROWS, D_MODEL), lambda i: (i, 0)),
                pl.BlockSpec((None, 6, D_MODEL), lambda i: (blk_row(i), 0, 0)),
                pl.BlockSpec((COMB_ROWS, TOP_K), lambda i: (i, 0))]
    args = [idx3, idx3, y_sorted, x_mid, mod_l, gates]
    if final is None:
        n_ctx_blk = None
        out_specs = pl.BlockSpec((COMB_ROWS, D_MODEL), lambda i: (i, 0))
        out_shape = jax.ShapeDtypeStruct((t, D_MODEL), F32)
    else:
        g_final, ctx_rows = final
        n_ctx_blk = ctx_rows // COMB_ROWS
        in_specs.append(pl.BlockSpec((1, D_MODEL), lambda i: (0, 0)))
        args.append(g_final.reshape(1, D_MODEL))
        out_specs = [pl.BlockSpec((COMB_ROWS, D_MODEL), lambda i: (jnp.minimum(i, n_ctx_blk - 1), 0)),
                     pl.BlockSpec((COMB_ROWS, D_MODEL), lambda i: (jnp.maximum(i - n_ctx_blk, 0), 0))]
        out_shape = [jax.ShapeDtypeStruct((ctx_rows, D_MODEL), F32),
                     jax.ShapeDtypeStruct((t - ctx_rows, D_MODEL), F32)]
    return pl.pallas_call(
        functools.partial(_combine_kernel, nblk=nblk, n_ctx_blk=n_ctx_blk),
        grid=(nblk,),
        in_specs=in_specs,
        out_specs=out_specs,
        out_shape=out_shape,
        scratch_shapes=[pltpu.VMEM((2, TOP_K, COMB_ROWS, D_MODEL), F32), pltpu.SemaphoreType.DMA((2,))],
        compiler_params=_cparams(1),
        name="moe_combine" if final is None else "moe_combine_final",
    )(*args)


def _rope_tables(n):
    rows = n // GRID_W
    row = jnp.repeat(jnp.arange(rows), GRID_W).astype(F32)
    col = jnp.tile(jnp.arange(GRID_W), rows).astype(F32)
    n_freq = HEAD_DIM // 4
    inv = ROPE_BASE ** (-jnp.arange(n_freq, dtype=F32) / n_freq)
    ang = jnp.stack([row[:, None] * inv, col[:, None] * inv], axis=1)
    cos, sin = jnp.cos(ang), jnp.sin(ang)
    c64 = jnp.stack([cos, cos], axis=2).reshape(n, HEAD_DIM)
    s64 = jnp.stack([-sin, sin], axis=2).reshape(n, HEAD_DIM)
    rep = LANES // HEAD_DIM
    return jnp.tile(c64, (1, rep)), jnp.tile(s64, (1, rep))


def _block_diag(w):
    eye = jnp.eye(B_HEADS, dtype=w.dtype)
    hd = B_W // B_HEADS
    full = w[:, :, :, None, :] * eye[None, :, None, :, None]
    return full.reshape(2, B_W, B_W).astype(BF16)


def kernel(x_prompt, x_sample, cache_k, cache_v, state_lru, c, c_ctx, w_mod, b_mod, g_mix, w_in, w_sp, b_sp, conv_w, conv_b, w_rg, b_rg, w_ig, b_ig, lru_lambda, sinks, w_out, g_ffn, w_router, b_router, w_gu, b_gu, w_down, b_down, g_final):
    nb_c, len_c, _ = x_prompt.shape
    nb_l, len_l, _ = x_sample.shape
    depth = w_mod.shape[0]
    tc, tl = nb_c * len_c, nb_l * len_l
    t = tc + tl
    assert len_c % TM == 0 and len_l % TM == 0 and len_l % GRID_W == 0
    assert len_c % LRU_CHUNK == 0 and len_l % LRU_CHUNK == 0 and tc % len_l == 0

    stream = (x_prompt.reshape(tc, D_MODEL), x_sample.reshape(tl, D_MODEL))
    n_rows = 1 + nb_l
    r_pad = -(-n_rows // SUBLANES) * SUBLANES
    cc = jnp.zeros((r_pad, D_MODEL), F32).at[0].set(c_ctx).at[1:n_rows].set(c)
    mod = _modulation(cc, w_mod, b_mod).reshape(depth, r_pad, 6, D_MODEL)

    def row_map(rows_per_blk):
        n_ctx_blk = tc // rows_per_blk
        per_seq = len_l // rows_per_blk
        return lambda i: jnp.where(i < n_ctx_blk, 0, 1 + (jnp.maximum(i - n_ctx_blk, 0) // per_seq))

    tile_row = row_map(TM)
    cos_t, sin_t = _rope_tables(len_l)
    w_in_bf = w_in.astype(BF16)
    w_out_bf = w_out.astype(BF16)
    h0_ctx = jnp.zeros((nb_c, 2, B_W), F32)

    ks, vs, ss = [], [], []
    for l in range(depth):
        mod_l = mod[l]
        proj = _inproj(stream, t, mod_l, g_mix[l], w_in_bf[l], cos_t, sin_t, tile_row, tc // TM, len_l // TM)
        a_out = _chunk_mlp(proj, w_sp[l], b_sp[l])
        lru_p = {"conv_w": conv_w[l], "conv_b": conv_b[l], "wr_bd": _block_diag(w_rg[l]), "b_rg": b_rg[l],
                 "wi_bd": _block_diag(w_ig[l]), "b_ig": b_ig[l], "lam": lru_lambda[l]}
        b_ctx, st_ctx = _lru(proj, h0_ctx, lru_p, 0, len_c)
        b_lat, _ = _lru(proj, state_lru[:, l].astype(F32), lru_p, tc, len_l)
        c_ctx_out = _ctx_attention(proj, sinks[l], nb_c, len_c)
        c_lat_out = _lat_attention(proj, sinks[l], cache_k, cache_v, l, tc, nb_l, len_l)
        x_mid, h2, top_e, gates_t, rank, counts = _outproj(
            a_out, b_ctx, b_lat, c_ctx_out, c_lat_out, stream, mod_l, w_out_bf[l], g_ffn[l], w_router[l],
            b_router[l], tile_row, tc // TM)
        dest, zflag, sched_gu, sched_dn = _route_tables(counts[:, 0], top_e, rank, t)
        x_sorted = _scatter_rows(h2, dest, zflag)
        y_sorted = _moe_experts(x_sorted, sched_gu, sched_dn, w_gu, b_gu, w_down, b_down, l)
        if l + 1 < depth:
            stream = (_combine(y_sorted, dest, gates_t, x_mid, mod_l, row_map(COMB_ROWS)),)
        else:
            y_ctx, y_lat = _combine(y_sorted, dest, gates_t, x_mid, mod_l, row_map(COMB_ROWS), final=(g_final, tc))

        kctx = proj[:tc, COL_K:COL_K + KV_W].reshape(nb_c, len_c, N_KV, HEAD_DIM).transpose(0, 2, 1, 3)
        vctx = proj[:tc, COL_V:COL_V + KV_W].reshape(nb_c, len_c, N_KV, HEAD_DIM).transpose(0, 2, 1, 3)
        ks.append(kctx)
        vs.append(vctx)
        ss.append(st_ctx)

    y_prompt = y_ctx.reshape(nb_c, len_c, D_MODEL)
    y_sample = y_lat.reshape(nb_l, len_l, D_MODEL)
    return (y_prompt, y_sample, jnp.stack(ks, axis=1), jnp.stack(vs, axis=1), jnp.stack(ss, axis=1))
```

```python
import functools

import jax
import jax.numpy as jnp
from jax import lax
from jax.experimental import pallas as pl
from jax.experimental.pallas import tpu as pltpu

F32 = jnp.float32
BF16 = jnp.bfloat16
I32 = jnp.int32

D_MODEL = 2048
A_W = 512
A_GROUPS = 4
CHUNK = 128
B_W = 512
B_HEADS = 8
CONV_W = 4
RG_C = 8.0
C_W = 1024
HEAD_DIM = 64
N_HEADS = 16
N_KV = 4
GQA_G = 4
KV_W = 256
WBLK = 128
GRID_W = 64
ROPE_BASE = 10000.0
ATTN_SCALE = HEAD_DIM ** -0.5
N_EXP = 32
TOP_K = 4
D_FF = 2048
SWIGLU_LIMIT = 7.0
SWIGLU_ALPHA = 1.702
EPS = 1e-6
IN_COLS = 2 * A_W + 2 * B_W + C_W + 2 * KV_W
COL_Q = 2 * A_W + 2 * B_W
COL_K = COL_Q + C_W
COL_V = COL_K + KV_W

LANES = 128
SUBLANES = 8
VMEM_LIMIT = 56 * 1024 * 1024
TM = 256
LRU_CHUNK = 256
MOE_TILE = 1024
MOE_CHK = 256
MOE_TN = 512
MOE_TN_DOWN = 1024
SCAT_ROWS = 256
COMB_ROWS = 256
HI_MASK = -65536


def _cparams(n_axes=1):
    return pltpu.CompilerParams(dimension_semantics=("arbitrary",) * n_axes, vmem_limit_bytes=VMEM_LIMIT)


def _mod_kernel(c_ref, w_ref, b_ref, o_ref):
    x = c_ref[...]
    s = (x * jax.nn.sigmoid(x)).astype(BF16)
    o_ref[...] = jnp.dot(s, w_ref[...].astype(BF16), preferred_element_type=F32) + b_ref[...]


def _modulation(cc, w_mod, b_mod):
    depth = w_mod.shape[0]
    r = cc.shape[0]
    tn = 1024
    return pl.pallas_call(
        _mod_kernel,
        grid=(depth, 6 * D_MODEL // tn),
        in_specs=[pl.BlockSpec((r, D_MODEL), lambda l, j: (0, 0)),
                  pl.BlockSpec((None, D_MODEL, tn), lambda l, j: (l, 0, j)),
                  pl.BlockSpec((None, 1, tn), lambda l, j: (l, 0, j))],
        out_specs=pl.BlockSpec((None, r, tn), lambda l, j: (l, 0, j)),
        out_shape=jax.ShapeDtypeStruct((depth, r, 6 * D_MODEL), F32),
        compiler_params=_cparams(2),
        name="modulation",
    )(cc, w_mod, b_mod.reshape(depth, 1, 6 * D_MODEL))


def _swap16(x):
    lane = lax.broadcasted_iota(I32, x.shape, 1)
    first = (lane % 32) < 16
    return jnp.where(first, pltpu.roll(x, LANES - 16, 1), pltpu.roll(x, 16, 1))


def _inproj_kernel(*refs, n_ctx_tiles, n_x):
    mod_ref, g_ref, w_ref, cos_ref, sin_ref, o_ref = refs[n_x:]
    x = _stream_tile(refs[:n_x], n_ctx_tiles)
    y = x * lax.rsqrt(jnp.mean(x * x, axis=-1, keepdims=True) + EPS) * g_ref[...]
    h = y * (1.0 + mod_ref[1:2, :]) + mod_ref[0:1, :]
    o_ref[...] = jnp.dot(h.astype(BF16), w_ref[...], preferred_element_type=F32)

    @pl.when(pl.program_id(0) >= n_ctx_tiles)
    def _():
        cos = cos_ref[...]
        sin = sin_ref[...]
        for j in range((C_W + KV_W) // LANES):
            c0 = COL_Q + j * LANES
            blk = o_ref[:, c0:c0 + LANES]
            o_ref[:, c0:c0 + LANES] = blk * cos + _swap16(blk) * sin


def _stream_specs(stream, n_ctx_tiles):
    if len(stream) == 1:
        return [pl.BlockSpec((TM, D_MODEL), lambda i: (i, 0))]
    return [pl.BlockSpec((TM, D_MODEL), lambda i: (jnp.minimum(i, n_ctx_tiles - 1), 0)),
            pl.BlockSpec((TM, D_MODEL), lambda i: (jnp.maximum(i - n_ctx_tiles, 0), 0))]


def _stream_tile(x_refs, n_ctx_tiles):
    if len(x_refs) == 1:
        return x_refs[0][...]
    return jnp.where(pl.program_id(0) < n_ctx_tiles, x_refs[0][...], x_refs[1][...])


def _inproj(stream, t, mod_l, g_mix, w_in_bf, cos_t, sin_t, tile_row, n_ctx_tiles, tiles_per_seq):
    def pos_map(i):
        return (jnp.maximum(i - n_ctx_tiles, 0) % tiles_per_seq, 0)

    return pl.pallas_call(
        functools.partial(_inproj_kernel, n_ctx_tiles=n_ctx_tiles, n_x=len(stream)),
        grid=(t // TM,),
        in_specs=_stream_specs(stream, n_ctx_tiles) + [
                  pl.BlockSpec((None, 6, D_MODEL), lambda i: (tile_row(i), 0, 0)),
                  pl.BlockSpec((1, D_MODEL), lambda i: (0, 0)),
                  pl.BlockSpec((D_MODEL, IN_COLS), lambda i: (0, 0)),
                  pl.BlockSpec((TM, LANES), pos_map),
                  pl.BlockSpec((TM, LANES), pos_map)],
        out_specs=pl.BlockSpec((TM, IN_COLS), lambda i: (i, 0)),
        out_shape=jax.ShapeDtypeStruct((t, IN_COLS), F32),
        compiler_params=_cparams(1),
        name="inproj",
    )(*stream, mod_l, g_mix.reshape(1, D_MODEL), w_in_bf, cos_t, sin_t)


def _chunk_mlp_kernel(u_ref, v_ref, w_ref, b_ref, o_ref):
    gd = A_W // A_GROUPS
    for c in range(TM // CHUNK):
        rows = slice(c * CHUNK, (c + 1) * CHUNK)
        for g in range(A_GROUPS):
            gu = jax.nn.gelu(u_ref[rows, g * gd:(g + 1) * gd])
            gv = jax.nn.gelu(v_ref[rows, g * gd:(g + 1) * gd]).astype(BF16)
            s = jnp.dot(w_ref[g].astype(BF16), gv, preferred_element_type=F32) + b_ref[:, g:g + 1]
            o_ref[rows, g * gd:(g + 1) * gd] = gu * s


def _chunk_mlp(proj, w_sp, b_sp):
    t = proj.shape[0]
    return pl.pallas_call(
        _chunk_mlp_kernel,
        grid=(t // TM,),
        in_specs=[pl.BlockSpec((TM, A_W), lambda i: (i, 0)),
                  pl.BlockSpec((TM, A_W), lambda i: (i, 1)),
                  pl.BlockSpec((A_GROUPS, CHUNK, CHUNK), lambda i: (0, 0, 0)),
                  pl.BlockSpec((CHUNK, A_GROUPS), lambda i: (0, 0))],
        out_specs=pl.BlockSpec((TM, A_W), lambda i: (i, 0)),
        out_shape=jax.ShapeDtypeStruct((t, A_W), F32),
        compiler_params=_cparams(1),
        name="chunk_mlp",
    )(proj, proj, w_sp, b_sp.T)


def _softplus(z):
    return jnp.maximum(z, 0.0) + jnp.log1p(jnp.exp(-jnp.abs(z)))


def _expm1(x):
    u = jnp.exp(x)
    um1 = u - 1.0
    safe = jnp.logical_and(um1 != 0.0, um1 != -1.0)
    return jnp.where(safe, um1 * x / jnp.log(jnp.where(safe, u, 2.0)), jnp.where(um1 == 0.0, x, -1.0))


def _lru_kernel(xr_ref, gr_ref, h0_ref, cw_ref, cb_ref, wr_ref, br_ref, wi_ref, bi_ref, lam_ref,
                y_ref, st_ref, a_s, b_s, *, seq_len):
    ch = LRU_CHUNK
    nch = seq_len // ch
    ngrp = ch // SUBLANES
    row = lax.broadcasted_iota(I32, (SUBLANES, B_W), 0)

    for d in (0, 1):
        sp = _softplus(-lam_ref[d:d + 1, :])
        cw = cw_ref[d]
        cb = cb_ref[d:d + 1, :]
        br = br_ref[d:d + 1, :]
        bi = bi_ref[d:d + 1, :]

        def chunk_body(ci, h, d=d, sp=sp, cw=cw, cb=cb, br=br, bi=bi):
            c = ci if d == 0 else nch - 1 - ci
            r0 = pl.multiple_of(c * ch, ch)
            xc = xr_ref[pl.ds(r0, ch), :]
            if d == 0:
                hs = pl.multiple_of(jnp.maximum(r0 - SUBLANES, 0), SUBLANES)
                halo = jnp.where(c > 0, xr_ref[pl.ds(hs, SUBLANES), :], 0.0)
                win = jnp.concatenate([halo, xc], axis=0)
                offs = [SUBLANES - (CONV_W - 1) + j for j in range(CONV_W)]
            else:
                hs = pl.multiple_of(jnp.minimum(r0 + ch, seq_len - SUBLANES), SUBLANES)
                halo = jnp.where(c < nch - 1, xr_ref[pl.ds(hs, SUBLANES), :], 0.0)
                win = jnp.concatenate([xc, halo], axis=0)
                offs = [CONV_W - 1 - j for j in range(CONV_W)]
            conv = cb
            for j in range(CONV_W):
                conv = conv + win[offs[j]:offs[j] + ch, :] * cw[j:j + 1, :]
            xb = conv.astype(BF16)
            r = jax.nn.sigmoid(jnp.dot(xb, wr_ref[d], preferred_element_type=F32) + br)
            ig = jax.nn.sigmoid(jnp.dot(xb, wi_ref[d], preferred_element_type=F32) + bi)
            log_a = -RG_C * r * sp
            a_s[...] = jnp.exp(log_a)
            b_s[...] = jnp.sqrt(-_expm1(2.0 * log_a)) * (ig * conv)

            def grp_body(gi, hc):
                g = gi if d == 0 else ngrp - 1 - gi
                q0 = pl.multiple_of(g * SUBLANES, SUBLANES)
                a8 = a_s[pl.ds(q0, SUBLANES), :]
                b8 = b_s[pl.ds(q0, SUBLANES), :]
                for dd in (1, 2, 4):
                    if d == 0:
                        keep = row >= dd
                        sh = dd
                    else:
                        keep = row < SUBLANES - dd
                        sh = SUBLANES - dd
                    a_sh = jnp.where(keep, pltpu.roll(a8, sh, 0), 1.0)
                    b_sh = jnp.where(keep, pltpu.roll(b8, sh, 0), 0.0)
                    b8 = a8 * b_sh + b8
                    a8 = a8 * a_sh
                h8 = a8 * hc + b8
                o0 = pl.multiple_of(r0 + q0, SUBLANES)
                if d == 0:
                    y_ref[pl.ds(o0, SUBLANES), :] = h8
                    return h8[SUBLANES - 1:SUBLANES, :]
                y_ref[pl.ds(o0, SUBLANES), :] = y_ref[pl.ds(o0, SUBLANES), :] + h8
                return h8[0:1, :]

            h = lax.fori_loop(0, ngrp, grp_body, h, unroll=2)
            if d == 1:
                y_ref[pl.ds(r0, ch), :] = jax.nn.gelu(gr_ref[pl.ds(r0, ch), :]) * y_ref[pl.ds(r0, ch), :]
            return h

        h_last = lax.fori_loop(0, nch, chunk_body, h0_ref[d:d + 1, :])
        st_ref[d:d + 1, :] = h_last


def _lru(proj, h0, p, row0, seq_len):
    n_seq = h0.shape[0]
    blk0 = row0 // seq_len
    w2 = pl.BlockSpec((2, B_W), lambda s: (0, 0))
    wbd = pl.BlockSpec((2, B_W, B_W), lambda s: (0, 0, 0))
    y, st = pl.pallas_call(
        functools.partial(_lru_kernel, seq_len=seq_len),
        grid=(n_seq,),
        in_specs=[pl.BlockSpec((seq_len, B_W), lambda s: (blk0 + s, 2)),
                  pl.BlockSpec((seq_len, B_W), lambda s: (blk0 + s, 3)),
                  pl.BlockSpec((None, 2, B_W), lambda s: (s, 0, 0)),
                  pl.BlockSpec((2, CONV_W, B_W), lambda s: (0, 0, 0)),
                  w2, wbd, w2, wbd, w2, w2],
        out_specs=[pl.BlockSpec((seq_len, B_W), lambda s: (s, 0)),
                   pl.BlockSpec((None, 2, B_W), lambda s: (s, 0, 0))],
        out_shape=[jax.ShapeDtypeStruct((n_seq * seq_len, B_W), F32), jax.ShapeDtypeStruct((n_seq, 2, B_W), F32)],
        scratch_shapes=[pltpu.VMEM((LRU_CHUNK, B_W), F32), pltpu.VMEM((LRU_CHUNK, B_W), F32)],
        compiler_params=_cparams(1),
        name=f"rglru_{seq_len}",
    )(proj, proj, h0, p["conv_w"], p["conv_b"], p["wr_bd"], p["b_rg"], p["wi_bd"], p["b_ig"], p["lam"])
    return y, st


def _attend(q_ref, pieces, sinks_ref, o_ref, nq):
    dn = (((1,), (1,)), ((), ()))
    bias = None
    if any(p[2] is not None for p in pieces):
        parts = [jnp.zeros((GQA_G * nq, p[3]), F32) if p[2] is None else jnp.where(p[2], 0.0, -jnp.inf)
                 for p in pieces]
        bias = jnp.concatenate(parts, axis=1)
    for kv in range(N_KV):
        heads = [kv * GQA_G + g for g in range(GQA_G)]
        qg = jnp.concatenate([q_ref[:, h * HEAD_DIM:(h + 1) * HEAD_DIM] for h in heads], axis=0)
        qg = (qg * ATTN_SCALE).astype(BF16)
        sink = jnp.concatenate([jnp.full((nq, 1), sinks_ref[h], F32) for h in heads], axis=0)
        k_all = jnp.concatenate([p[0](kv) for p in pieces], axis=0)
        v_all = jnp.concatenate([p[1](kv) for p in pieces], axis=0)
        s = lax.dot_general(qg, k_all, dn, preferred_element_type=F32)
        if bias is not None:
            s = s + bias
        m = jnp.maximum(sink, jnp.max(s, axis=-1, keepdims=True))
        e = jnp.exp(s - m)
        den = jnp.exp(sink - m) + jnp.sum(e, axis=-1, keepdims=True)
        o = jnp.dot(e.astype(BF16), v_all, preferred_element_type=F32) / den
        for g, h in enumerate(heads):
            o_ref[:, h * HEAD_DIM:(h + 1) * HEAD_DIM] = o[g * nq:(g + 1) * nq, :]


def _ctx_attn_kernel(sinks_ref, q_ref, k_ref, v_ref, o_ref, *, nq):
    def k_fn(kv):
        return k_ref[:, kv * HEAD_DIM:(kv + 1) * HEAD_DIM].astype(BF16)

    def v_fn(kv):
        return v_ref[:, kv * HEAD_DIM:(kv + 1) * HEAD_DIM].astype(BF16)

    _attend(q_ref, [(k_fn, v_fn, None, nq)], sinks_ref, o_ref, nq)


def _ctx_attention(proj, sinks, n_seq, seq_len):
    smem = pl.BlockSpec(memory_space=pltpu.SMEM)
    return pl.pallas_call(
        functools.partial(_ctx_attn_kernel, nq=seq_len),
        grid=(n_seq,),
        in_specs=[smem,
                  pl.BlockSpec((seq_len, C_W), lambda s: (s, COL_Q // C_W)),
                  pl.BlockSpec((seq_len, KV_W), lambda s: (s, COL_K // KV_W)),
                  pl.BlockSpec((seq_len, KV_W), lambda s: (s, COL_V // KV_W))],
        out_specs=pl.BlockSpec((seq_len, C_W), lambda s: (s, 0)),
        out_shape=jax.ShapeDtypeStruct((n_seq * seq_len, C_W), F32),
        compiler_params=_cparams(1),
        name="ctx_attention",
    )(sinks, proj, proj, proj)


def _lat_attn_kernel(sinks_ref, q_ref, kp_ref, kc_ref, kn_ref, vp_ref, vc_ref, vn_ref, ck_ref, cv_ref,
                     o_ref, *, nblk):
    n = pl.program_id(1)
    nq = WBLK
    rows = lax.broadcasted_iota(I32, (GQA_G * nq, WBLK), 0) % nq
    cols = lax.broadcasted_iota(I32, (GQA_G * nq, WBLK), 1)
    mask_prev = jnp.logical_and(cols >= rows, n > 0)
    mask_next = jnp.logical_and(cols <= rows, n < nblk - 1)

    def loc(ref):
        return lambda kv: ref[:, kv * HEAD_DIM:(kv + 1) * HEAD_DIM].astype(BF16)

    def cache(ref):
        return lambda kv: ref[kv].astype(BF16)

    pieces = [(loc(kp_ref), loc(vp_ref), mask_prev, WBLK),
              (loc(kc_ref), loc(vc_ref), None, WBLK),
              (loc(kn_ref), loc(vn_ref), mask_next, WBLK),
              (cache(ck_ref), cache(cv_ref), None, ck_ref.shape[1])]
    _attend(q_ref, pieces, sinks_ref, o_ref, nq)


def _lat_attention(proj, sinks, cache_k, cache_v, layer, row0, n_seq, seq_len):
    nblk = seq_len // WBLK
    b0 = row0 // WBLK
    past = cache_k.shape[3]
    smem = pl.BlockSpec(memory_space=pltpu.SMEM)
    kcol = COL_K // KV_W
    vcol = COL_V // KV_W

    def blk(col, off):
        return pl.BlockSpec((WBLK, KV_W), lambda b, n: (b0 + b * nblk + jnp.clip(n + off, 0, nblk - 1), col))

    cache_spec = pl.BlockSpec((None, None, N_KV, past, HEAD_DIM), lambda b, n: (b, layer, 0, 0, 0))
    return pl.pallas_call(
        functools.partial(_lat_attn_kernel, nblk=nblk),
        grid=(n_seq, nblk),
        in_specs=[smem,
                  pl.BlockSpec((WBLK, C_W), lambda b, n: (b0 + b * nblk + n, COL_Q // C_W)),
                  blk(kcol, -1), blk(kcol, 0), blk(kcol, 1),
                  blk(vcol, -1), blk(vcol, 0), blk(vcol, 1),
                  cache_spec, cache_spec],
        out_specs=pl.BlockSpec((WBLK, C_W), lambda b, n: (b * nblk + n, 0)),
        out_shape=jax.ShapeDtypeStruct((n_seq * seq_len, C_W), F32),
        compiler_params=_cparams(2),
        name="lat_attention",
    )(sinks, proj, proj, proj, proj, proj, proj, proj, cache_k, cache_v)


def _outproj_kernel(*refs, n_ctx_tiles, n_x):
    (a_ref, bc_ref, bl_ref, cc_ref, cl_ref, mod_ref, w_ref, g_ref, wrh_ref, wrl_ref, br_ref,
     xo_ref, h2_ref, te_ref, tg_ref, rk_ref, cnt_ref, carry) = refs[n_x:]
    i = pl.program_id(0)
    is_ctx = i < n_ctx_tiles
    b_mix = jnp.where(is_ctx, bc_ref[...], bl_ref[...])
    c_mix = jnp.where(is_ctx, cc_ref[...], cl_ref[...])
    x_in = _stream_tile(refs[:n_x], n_ctx_tiles)

    @pl.when(i == 0)
    def _():
        carry[...] = jnp.zeros_like(carry)

    mixed = jnp.concatenate([a_ref[...], b_mix, c_mix], axis=-1).astype(BF16)
    mix = jnp.dot(mixed, w_ref[...], preferred_element_type=F32)
    x = x_in + mod_ref[2:3, :] * mix
    xo_ref[...] = x
    y = x * lax.rsqrt(jnp.mean(x * x, axis=-1, keepdims=True) + EPS) * g_ref[...]
    h2 = y * (1.0 + mod_ref[4:5, :]) + mod_ref[3:4, :]
    h2_ref[...] = _pack_bf16_pairs(h2)

    dn = (((1,), (1,)), ((), ()))
    h_hi = h2.astype(BF16)
    h_lo = (h2 - h_hi.astype(F32)).astype(BF16)
    wh = wrh_ref[...]
    logits = (lax.dot_general(wh, h_hi, dn, preferred_element_type=F32)
              + lax.dot_general(wh, h_lo, dn, preferred_element_type=F32)
              + lax.dot_general(wrl_ref[...], h_hi, dn, preferred_element_type=F32)) + br_ref[...]

    eidx = lax.broadcasted_iota(I32, (N_EXP, TM), 0).astype(F32)
    l = logits
    vals, hots = [], []
    for k in range(TOP_K):
        m = jnp.max(l, axis=0, keepdims=True)
        sel = jnp.min(jnp.where(l == m, eidx, float(N_EXP)), axis=0, keepdims=True)
        hot = eidx == sel
        vals.append(m)
        hots.append(hot)
        te_ref[k:k + 1, :] = sel.astype(I32)
        l = jnp.where(hot, -jnp.inf, l)
    exps = [jnp.exp(v - vals[0]) for v in vals]
    den = exps[0] + exps[1] + exps[2] + exps[3]
    for k in range(TOP_K):
        tg_ref[k:k + 1, :] = exps[k] / den

    any_hot = jnp.logical_or(jnp.logical_or(hots[0], hots[1]), jnp.logical_or(hots[2], hots[3]))
    hot_f = any_hot.astype(F32)
    tri = (lax.broadcasted_iota(I32, (TM, TM), 0) < lax.broadcasted_iota(I32, (TM, TM), 1)).astype(BF16)
    pos = jnp.dot(hot_f.astype(BF16), tri, preferred_element_type=F32) + carry[...]
    for k in range(TOP_K):
        rk_ref[k:k + 1, :] = jnp.sum(jnp.where(hots[k], pos, 0.0), axis=0, keepdims=True).astype(I32)
    carry[...] = carry[...] + jnp.sum(hot_f, axis=1, keepdims=True)
    cnt_ref[...] = carry[...].astype(I32)


def _outproj(a_out, b_ctx, b_lat, c_ctx, c_lat, stream, mod_l, w_out_bf, g_ffn, w_router, b_router, tile_row,
             n_ctx_tiles):
    t = a_out.shape[0]

    def ctx_map(i):
        return (jnp.minimum(i, n_ctx_tiles - 1), 0)

    def lat_map(i):
        return (jnp.maximum(i - n_ctx_tiles, 0), 0)

    wr_t = w_router.T
    wr_hi = wr_t.astype(BF16)
    wr_lo = (wr_t - wr_hi.astype(F32)).astype(BF16)
    row4 = pl.BlockSpec((TOP_K, TM), lambda i: (0, i))
    return pl.pallas_call(
        functools.partial(_outproj_kernel, n_ctx_tiles=n_ctx_tiles, n_x=len(stream)),
        grid=(t // TM,),
        in_specs=_stream_specs(stream, n_ctx_tiles) + [
                  pl.BlockSpec((TM, A_W), lambda i: (i, 0)),
                  pl.BlockSpec((TM, B_W), ctx_map),
                  pl.BlockSpec((TM, B_W), lat_map),
                  pl.BlockSpec((TM, C_W), ctx_map),
                  pl.BlockSpec((TM, C_W), lat_map),
                  pl.BlockSpec((None, 6, D_MODEL), lambda i: (tile_row(i), 0, 0)),
                  pl.BlockSpec((D_MODEL, D_MODEL), lambda i: (0, 0)),
                  pl.BlockSpec((1, D_MODEL), lambda i: (0, 0)),
                  pl.BlockSpec((N_EXP, D_MODEL), lambda i: (0, 0)),
                  pl.BlockSpec((N_EXP, D_MODEL), lambda i: (0, 0)),
                  pl.BlockSpec((N_EXP, 1), lambda i: (0, 0))],
        out_specs=[pl.BlockSpec((TM, D_MODEL), lambda i: (i, 0)),
                   pl.BlockSpec((TM, D_MODEL // 2), lambda i: (i, 0)),
                   row4, row4, row4,
                   pl.BlockSpec((N_EXP, 1), lambda i: (0, 0))],
        out_shape=[jax.ShapeDtypeStruct((t, D_MODEL), F32),
                   jax.ShapeDtypeStruct((t, D_MODEL // 2), I32),
                   jax.ShapeDtypeStruct((TOP_K, t), I32),
                   jax.ShapeDtypeStruct((TOP_K, t), F32),
                   jax.ShapeDtypeStruct((TOP_K, t), I32),
                   jax.ShapeDtypeStruct((N_EXP, 1), I32)],
        scratch_shapes=[pltpu.VMEM((N_EXP, 1), F32)],
        compiler_params=_cparams(1),
        name="outproj_router",
    )(*stream, a_out, b_ctx, b_lat, c_ctx, c_lat, mod_l, w_out_bf, g_ffn.reshape(1, D_MODEL),
      wr_hi, wr_lo, b_router.reshape(N_EXP, 1))


def _schedule(nn, ntile_e, tile_start, tile_end, nch_tile, n_tiles):
    n_steps = nn * n_tiles
    steps_e = nn * ntile_e
    s_end = jnp.cumsum(steps_e)
    s_start = s_end - steps_e
    total = s_end[-1]
    sidx = jnp.arange(n_steps, dtype=I32)
    valid = sidx < total
    last = jnp.maximum(total - 1, 0)
    sidx_c = jnp.where(valid, sidx, last)
    se = jnp.minimum(jnp.sum(s_end[None, :] <= sidx_c[:, None], axis=1), N_EXP - 1).astype(I32)
    loc = sidx_c - s_start[se]
    nt = jnp.maximum(ntile_e[se], 1)
    sj = loc % nt
    used = tile_end[-1]
    n_unused = jnp.maximum(n_tiles - used, 1)
    extra = jnp.maximum(sidx - total, 0)
    sn = jnp.where(valid, loc // nt, extra // n_unused).astype(I32)
    stile = jnp.where(valid, tile_start[se] + (nt - 1 - sj), used + extra % n_unused)
    stile = jnp.clip(stile, 0, n_tiles - 1).astype(I32)
    snch = jnp.where(valid, nch_tile[stile], 0).astype(I32)
    first = jnp.logical_and(valid, sj == 0).astype(I32)
    return (se, sn, stile, snch, first)


def _route_tables(counts, top_e, rank, t):
    r_tot = t * TOP_K + N_EXP * MOE_TILE
    n_tiles = r_tot // MOE_TILE
    chunks_per_tile = MOE_TILE // MOE_CHK

    ntile_e = (counts + MOE_TILE - 1) // MOE_TILE
    tile_end = jnp.cumsum(ntile_e)
    tile_start = tile_end - ntile_e
    reg_start = tile_start * MOE_TILE

    eids = jnp.arange(N_EXP, dtype=I32)[:, None, None]
    dest = jnp.sum(jnp.where(top_e[None] == eids, reg_start[:, None, None], 0), axis=0) + rank

    tidx = jnp.arange(n_tiles, dtype=I32)
    te = jnp.sum(tile_end[None, :] <= tidx[:, None], axis=1).astype(I32)
    te_c = jnp.minimum(te, N_EXP - 1)
    left = jnp.where(te < N_EXP, counts[te_c] - (tidx - tile_start[te_c]) * MOE_TILE, 0)
    nch_tile = jnp.clip((left + MOE_CHK - 1) // MOE_CHK, 0, chunks_per_tile).astype(I32)

    bidx = jnp.arange(r_tot // MOE_CHK, dtype=I32)
    real_rows = left[bidx // chunks_per_tile] - (bidx % chunks_per_tile) * MOE_CHK
    zflag = (real_rows < MOE_CHK).astype(I32)

    sched_gu = _schedule(D_FF // MOE_TN, ntile_e, tile_start, tile_end, nch_tile, n_tiles)
    sched_dn = _schedule(D_MODEL // MOE_TN_DOWN, ntile_e, tile_start, tile_end, nch_tile, n_tiles)
    return dest, zflag, sched_gu, sched_dn


def _pack_bf16_pairs(h):
    half = h.shape[1] // 2
    lo = lax.bitcast_convert_type(h[:, :half].astype(BF16).astype(F32), I32)
    hi = lax.bitcast_convert_type(h[:, half:].astype(BF16).astype(F32), I32)
    return jnp.bitwise_or(jnp.bitwise_and(hi, HI_MASK), lax.shift_right_logical(lo, 16))


def _unpack_bf16_pairs(p):
    lo = lax.bitcast_convert_type(lax.shift_left(p, 16), F32).astype(BF16)
    hi = lax.bitcast_convert_type(jnp.bitwise_and(p, HI_MASK), F32).astype(BF16)
    return lo, hi


def _scatter_kernel(zflag_ref, dest_ref, h_ref, xs_hbm, buf, zbuf, sem, zsem, *, nblk, nzb):
    i = pl.program_id(0)
    slot = i % 2

    def zero_copy(b):
        r0 = pl.multiple_of(b * MOE_CHK, MOE_CHK)
        return pltpu.make_async_copy(zbuf, xs_hbm.at[pl.ds(r0, MOE_CHK)], zsem)

    @pl.when(i == 0)
    def _():
        zbuf[...] = jnp.zeros_like(zbuf)

        def zstart(b, carry):
            @pl.when(zflag_ref[b] > 0)
            def _():
                zero_copy(b).start()
            return carry

        def zwait(b, carry):
            @pl.when(zflag_ref[b] > 0)
            def _():
                zero_copy(b).wait()
            return carry

        lax.fori_loop(0, nzb, zstart, 0)
        lax.fori_loop(0, nzb, zwait, 0)

    def wait_rows(s):
        def body(r, carry):
            pltpu.make_async_copy(buf.at[s, pl.ds(0, 1)], xs_hbm.at[pl.ds(0, 1)], sem.at[s]).wait()
            return carry
        lax.fori_loop(0, TOP_K * SCAT_ROWS, body, 0, unroll=16)

    @pl.when(i >= 2)
    def _():
        wait_rows(slot)

    packed = h_ref[...]

    for s in (0, 1):
        @pl.when(slot == s)
        def _(s=s):
            buf[s] = packed
            for r in range(SCAT_ROWS):
                for k in range(TOP_K):
                    pltpu.make_async_copy(buf.at[s, pl.ds(r, 1)], xs_hbm.at[pl.ds(dest_ref[k, r], 1)],
                                          sem.at[s]).start()

    @pl.when(i == nblk - 1)
    def _():
        wait_rows(slot)
        if nblk > 1:
            wait_rows(1 - slot)


def _scatter_rows(h2, dest, zflag):
    t = h2.shape[0]
    r_tot = zflag.shape[0] * MOE_CHK
    nblk = t // SCAT_ROWS
    idx3 = dest.reshape(TOP_K, nblk, SCAT_ROWS).transpose(1, 0, 2)
    half = D_MODEL // 2
    grid_spec = pltpu.PrefetchScalarGridSpec(
        num_scalar_prefetch=1,
        grid=(nblk,),
        in_specs=[pl.BlockSpec((None, TOP_K, SCAT_ROWS), lambda i, z: (i, 0, 0), memory_space=pltpu.SMEM),
                  pl.BlockSpec((SCAT_ROWS, D_MODEL // 2), lambda i, z: (i, 0))],
        out_specs=pl.BlockSpec(memory_space=pl.ANY),
        scratch_shapes=[pltpu.VMEM((2, SCAT_ROWS, half), I32), pltpu.VMEM((MOE_CHK, half), I32),
                        pltpu.SemaphoreType.DMA((2,)), pltpu.SemaphoreType.DMA(())],
    )
    return pl.pallas_call(
        functools.partial(_scatter_kernel, nblk=nblk, nzb=zflag.shape[0]),
        grid_spec=grid_spec,
        out_shape=jax.ShapeDtypeStruct((r_tot, half), I32),
        compiler_params=_cparams(1),
        name="moe_scatter",
    )(zflag, idx3, h2)


def _for_valid_chunks(nch, rows_fn):
    full = MOE_TILE // MOE_CHK
    big = 2 * MOE_CHK

    @pl.when(nch == full)
    def _():
        rows_fn(0, MOE_TILE)

    @pl.when(nch != full)
    def _():
        npair = nch // 2

        def pair(c, carry):
            rows_fn(pl.multiple_of(c * big, big), big)
            return carry

        lax.fori_loop(0, npair, pair, 0)

        @pl.when(nch % 2 == 1)
        def _():
            rows_fn(pl.multiple_of(npair * big, MOE_CHK), MOE_CHK)


def _gmm_gu_kernel(e_ref, n_ref, t_ref, nch_ref, first_ref, x_ref, wg_ref, wu_ref, bg_ref, bu_ref,
                   o_ref, wg_s, wu_s):
    i = pl.program_id(0)
    del e_ref, n_ref, t_ref

    @pl.when(first_ref[i] > 0)
    def _():
        wg_s[...] = wg_ref[...].astype(BF16)
        wu_s[...] = wu_ref[...].astype(BF16)

    nch = nch_ref[i]

    def rows_fn(r0, rows):
        half = D_MODEL // 2
        x_lo, x_hi = _unpack_bf16_pairs(x_ref[pl.ds(r0, rows), :])
        g = (jnp.dot(x_lo, wg_s[:half, :], preferred_element_type=F32)
             + jnp.dot(x_hi, wg_s[half:, :], preferred_element_type=F32)) + bg_ref[...]
        u = (jnp.dot(x_lo, wu_s[:half, :], preferred_element_type=F32)
             + jnp.dot(x_hi, wu_s[half:, :], preferred_element_type=F32)) + bu_ref[...]
        g = jnp.minimum(g, SWIGLU_LIMIT)
        u = jnp.clip(u, -SWIGLU_LIMIT, SWIGLU_LIMIT)
        act = (u + 1.0) * (g * jax.nn.sigmoid(SWIGLU_ALPHA * g))
        o_ref[pl.ds(r0, rows), :] = act.astype(BF16)

    _for_valid_chunks(nch, rows_fn)

    def zero(c, carry):
        r0 = pl.multiple_of(c * MOE_CHK, MOE_CHK)
        o_ref[pl.ds(r0, MOE_CHK), :] = jnp.zeros((MOE_CHK, MOE_TN), BF16)
        return carry

    lax.fori_loop(nch, MOE_TILE // MOE_CHK, zero, 0)


def _gmm_down_kernel(e_ref, n_ref, t_ref, nch_ref, first_ref, a_ref, w_ref, b_ref, o_ref, w_s):
    i = pl.program_id(0)
    del e_ref, n_ref, t_ref

    @pl.when(first_ref[i] > 0)
    def _():
        w_s[...] = w_ref[...].astype(BF16)

    nch = nch_ref[i]

    def rows_fn(r0, rows):
        y = jnp.dot(a_ref[pl.ds(r0, rows), :], w_s[...], preferred_element_type=F32) + b_ref[...]
        o_ref[pl.ds(r0, rows), :] = y

    _for_valid_chunks(nch, rows_fn)

    def zero(c, carry):
        r0 = pl.multiple_of(c * MOE_CHK, MOE_CHK)
        o_ref[pl.ds(r0, MOE_CHK), :] = jnp.zeros((MOE_CHK, MOE_TN_DOWN), F32)
        return carry

    lax.fori_loop(nch, MOE_TILE // MOE_CHK, zero, 0)


def _moe_experts(x_sorted, sched_gu, sched_dn, w_gu, b_gu, w_down, b_down, layer):
    r_tot = x_sorted.shape[0]
    nn = D_FF // MOE_TN
    depth = w_gu.shape[0]
    b_gu4 = b_gu.reshape(depth, N_EXP, 1, 2 * D_FF)
    b_dn4 = b_down.reshape(depth, N_EXP, 1, D_MODEL)

    def wspec(rows, tn, off):
        return pl.BlockSpec((None, None, rows, tn), lambda i, e, n, t, c, f: (layer, e[i], 0, n[i] + off))

    gu_spec = pltpu.PrefetchScalarGridSpec(
        num_scalar_prefetch=5,
        grid=(sched_gu[0].shape[0],),
        in_specs=[pl.BlockSpec((MOE_TILE, D_MODEL // 2), lambda i, e, n, t, c, f: (t[i], 0)),
                  wspec(D_MODEL, MOE_TN, 0), wspec(D_MODEL, MOE_TN, nn), wspec(1, MOE_TN, 0), wspec(1, MOE_TN, nn)],
        out_specs=pl.BlockSpec((MOE_TILE, MOE_TN), lambda i, e, n, t, c, f: (t[i], n[i])),
        scratch_shapes=[pltpu.VMEM((D_MODEL, MOE_TN), BF16), pltpu.VMEM((D_MODEL, MOE_TN), BF16)],
    )
    act = pl.pallas_call(
        _gmm_gu_kernel,
        grid_spec=gu_spec,
        out_shape=jax.ShapeDtypeStruct((r_tot, D_FF), BF16),
        compiler_params=_cparams(1),
        name="moe_gate_up",
    )(*sched_gu, x_sorted, w_gu, w_gu, b_gu4, b_gu4)

    dn_spec = pltpu.PrefetchScalarGridSpec(
        num_scalar_prefetch=5,
        grid=(sched_dn[0].shape[0],),
        in_specs=[pl.BlockSpec((MOE_TILE, D_FF), lambda i, e, n, t, c, f: (t[i], 0)),
                  wspec(D_FF, MOE_TN_DOWN, 0), wspec(1, MOE_TN_DOWN, 0)],
        out_specs=pl.BlockSpec((MOE_TILE, MOE_TN_DOWN), lambda i, e, n, t, c, f: (t[i], n[i])),
        scratch_shapes=[pltpu.VMEM((D_FF, MOE_TN_DOWN), BF16)],
    )
    return pl.pallas_call(
        _gmm_down_kernel,
        grid_spec=dn_spec,
        out_shape=jax.ShapeDtypeStruct((r_tot, D_MODEL), F32),
        compiler_params=_cparams(1),
        name="moe_down",
    )(*sched_dn, act, w_down, b_dn4)


def _combine_kernel(idx_cur, idx_nxt, y_hbm, x_ref, mod_ref, g_ref, *rest, nblk, n_ctx_blk):
    if n_ctx_blk is None:
        o_ref, buf, sem = rest
    else:
        gf_ref, oc_ref, ol_ref, buf, sem = rest
    i = pl.program_id(0)

    def row_copy(idx_ref, k, r, slot):
        return pltpu.make_async_copy(y_hbm.at[pl.ds(idx_ref[k, r], 1)], buf.at[slot, k, pl.ds(r, 1)], sem.at[slot])

    def issue(idx_ref, slot):
        def body(r, carry):
            for k in range(TOP_K):
                row_copy(idx_ref, k, r, slot).start()
            return carry
        lax.fori_loop(0, COMB_ROWS, body, 0, unroll=4)

    @pl.when(i == 0)
    def _():
        issue(idx_cur, 0)

    for s in (0, 1):
        @pl.when(jnp.logical_and(i + 1 < nblk, (i + 1) % 2 == s))
        def _(s=s):
            for r in range(COMB_ROWS):
                for k in range(TOP_K):
                    row_copy(idx_nxt, k, r, s).start()

    slot = i % 2

    def wbody(r, carry):
        pltpu.make_async_copy(y_hbm.at[pl.ds(0, 1)], buf.at[slot, 0, pl.ds(0, 1)], sem.at[slot]).wait()
        return carry
    lax.fori_loop(0, TOP_K * COMB_ROWS, wbody, 0, unroll=16)

    g = g_ref[...]
    moe = buf[slot, 0] * g[:, 0:1]
    for k in range(1, TOP_K):
        moe = moe + buf[slot, k] * g[:, k:k + 1]
    x = x_ref[...] + mod_ref[5:6, :] * moe
    if n_ctx_blk is None:
        o_ref[...] = x
    else:
        y = x * lax.rsqrt(jnp.mean(x * x, axis=-1, keepdims=True) + EPS) * gf_ref[...]

        @pl.when(i < n_ctx_blk)
        def _():
            oc_ref[...] = y

        @pl.when(i >= n_ctx_blk)
        def _():
            ol_ref[...] = y


def _combine(y_sorted, dest, gates_t, x_mid, mod_l, blk_row, final=None):
    t = x_mid.shape[0]
    nblk = t // COMB_ROWS
    idx3 = dest.reshape(TOP_K, nblk, COMB_ROWS).transpose(1, 0, 2)
    gates = gates_t.T
    in_specs = [pl.BlockSpec((None, TOP_K, COMB_ROWS), lambda i: (i, 0, 0), memory_space=pltpu.SMEM),
                pl.BlockSpec((None, TOP_K, COMB_ROWS), lambda i: (jnp.minimum(i + 1, nblk - 1), 0, 0),
                             memory_space=pltpu.SMEM),
                pl.BlockSpec(memory_space=pl.ANY),
                pl.BlockSpec((COMB_ROWS, D_MODEL), lambda i: (i, 0)),
                pl.BlockSpec((None, 6, D_MODEL), lambda i: (blk_row(i), 0, 0)),
                pl.BlockSpec((COMB_ROWS, TOP_K), lambda i: (i, 0))]
    args = [idx3, idx3, y_sorted, x_mid, mod_l, gates]
    if final is None:
        n_ctx_blk = None
        out_specs = pl.BlockSpec((COMB_ROWS, D_MODEL), lambda i: (i, 0))
        out_shape = jax.ShapeDtypeStruct((t, D_MODEL), F32)
    else:
        g_final, ctx_rows = final
        n_ctx_blk = ctx_rows // COMB_ROWS
        in_specs.append(pl.BlockSpec((1, D_MODEL), lambda i: (0, 0)))
        args.append(g_final.reshape(1, D_MODEL))
        out_specs = [pl.BlockSpec((COMB_ROWS, D_MODEL), lambda i: (jnp.minimum(i, n_ctx_blk - 1), 0)),
                     pl.BlockSpec((COMB_ROWS, D_MODEL), lambda i: (jnp.maximum(i - n_ctx_blk, 0), 0))]
        out_shape = [jax.ShapeDtypeStruct((ctx_rows, D_MODEL), F32),
                     jax.ShapeDtypeStruct((t - ctx_rows, D_MODEL), F32)]
    return pl.pallas_call(
        functools.partial(_combine_kernel, nblk=nblk, n_ctx_blk=n_ctx_blk),
        grid=(nblk,),
        in_specs=in_specs,
        out_specs=out_specs,
        out_shape=out_shape,
        scratch_shapes=[pltpu.VMEM((2, TOP_K, COMB_ROWS, D_MODEL), F32), pltpu.SemaphoreType.DMA((2,))],
        compiler_params=_cparams(1),
        name="moe_combine" if final is None else "moe_combine_final",
    )(*args)


def _rope_tables(n):
    rows = n // GRID_W
    row = jnp.repeat(jnp.arange(rows), GRID_W).astype(F32)
    col = jnp.tile(jnp.arange(GRID_W), rows).astype(F32)
    n_freq = HEAD_DIM // 4
    inv = ROPE_BASE ** (-jnp.arange(n_freq, dtype=F32) / n_freq)
    ang = jnp.stack([row[:, None] * inv, col[:, None] * inv], axis=1)
    cos, sin = jnp.cos(ang), jnp.sin(ang)
    c64 = jnp.stack([cos, cos], axis=2).reshape(n, HEAD_DIM)
    s64 = jnp.stack([-sin, sin], axis=2).reshape(n, HEAD_DIM)
    rep = LANES // HEAD_DIM
    return jnp.tile(c64, (1, rep)), jnp.tile(s64, (1, rep))


def _block_diag(w):
    eye = jnp.eye(B_HEADS, dtype=w.dtype)
    hd = B_W // B_HEADS
    full = w[:, :, :, None, :] * eye[None, :, None, :, None]
    return full.reshape(2, B_W, B_W).astype(BF16)


def kernel(x_prompt, x_sample, cache_k, cache_v, state_lru, c, c_ctx, w_mod, b_mod, g_mix, w_in, w_sp, b_sp, conv_w, conv_b, w_rg, b_rg, w_ig, b_ig, lru_lambda, sinks, w_out, g_ffn, w_router, b_router, w_gu, b_gu, w_down, b_down, g_final):
    nb_c, len_c, _ = x_prompt.shape
    nb_l, len_l, _ = x_sample.shape
    depth = w_mod.shape[0]
    tc, tl = nb_c * len_c, nb_l * len_l
    t = tc + tl
    assert len_c % TM == 0 and len_l % TM == 0 and len_l % GRID_W == 0
    assert len_c % LRU_CHUNK == 0 and len_l % LRU_CHUNK == 0 and tc % len_l == 0

    stream = (x_prompt.reshape(tc, D_MODEL), x_sample.reshape(tl, D_MODEL))
    n_rows = 1 + nb_l
    r_pad = -(-n_rows // SUBLANES) * SUBLANES
    cc = jnp.zeros((r_pad, D_MODEL), F32).at[0].set(c_ctx).at[1:n_rows].set(c)
    mod = _modulation(cc, w_mod, b_mod).reshape(depth, r_pad, 6, D_MODEL)

    def row_map(rows_per_blk):
        n_ctx_blk = tc // rows_per_blk
        per_seq = len_l // rows_per_blk
        return lambda i: jnp.where(i < n_ctx_blk, 0, 1 + (jnp.maximum(i - n_ctx_blk, 0) // per_seq))

    tile_row = row_map(TM)
    cos_t, sin_t = _rope_tables(len_l)
    w_in_bf = w_in.astype(BF16)
    w_out_bf = w_out.astype(BF16)
    h0_ctx = jnp.zeros((nb_c, 2, B_W), F32)

    ks, vs, ss = [], [], []
    for l in range(depth):
        mod_l = mod[l]
        proj = _inproj(stream, t, mod_l, g_mix[l], w_in_bf[l], cos_t, sin_t, tile_row, tc // TM, len_l // TM)
        a_out = _chunk_mlp(proj, w_sp[l], b_sp[l])
        lru_p = {"conv_w": conv_w[l], "conv_b": conv_b[l], "wr_bd": _block_diag(w_rg[l]), "b_rg": b_rg[l],
                 "wi_bd": _block_diag(w_ig[l]), "b_ig": b_ig[l], "lam": lru_lambda[l]}
        b_ctx, st_ctx = _lru(proj, h0_ctx, lru_p, 0, len_c)
        b_lat, _ = _lru(proj, state_lru[:, l].astype(F32), lru_p, tc, len_l)
        c_ctx_out = _ctx_attention(proj, sinks[l], nb_c, len_c)
        c_lat_out = _lat_attention(proj, sinks[l], cache_k, cache_v, l, tc, nb_l, len_l)
        x_mid, h2, top_e, gates_t, rank, counts = _outproj(
            a_out, b_ctx, b_lat, c_ctx_out, c_lat_out, stream, mod_l, w_out_bf[l], g_ffn[l], w_router[l],
            b_router[l], tile_row, tc // TM)
        dest, zflag, sched_gu, sched_dn = _route_tables(counts[:, 0], top_e, rank, t)
        x_sorted = _scatter_rows(h2, dest, zflag)
        y_sorted = _moe_experts(x_sorted, sched_gu, sched_dn, w_gu, b_gu, w_down, b_down, l)
        if l + 1 < depth:
            stream = (_combine(y_sorted, dest, gates_t, x_mid, mod_l, row_map(COMB_ROWS)),)
        else:
            y_ctx, y_lat = _combine(y_sorted, dest, gates_t, x_mid, mod_l, row_map(COMB_ROWS), final=(g_final, tc))

        kctx = proj[:tc, COL_K:COL_K + KV_W].reshape(nb_c, len_c, N_KV, HEAD_DIM).transpose(0, 2, 1, 3)
        vctx = proj[:tc, COL_V:COL_V + KV_W].reshape(nb_c, len_c, N_KV, HEAD_DIM).transpose(0, 2, 1, 3)
        ks.append(kctx)
        vs.append(vctx)
        ss.append(st_ctx)

    y_prompt = y_ctx.reshape(nb_c, len_c, D_MODEL)
    y_sample = y_lat.reshape(nb_l, len_l, D_MODEL)
    return (y_prompt, y_sample, jnp.stack(ks, axis=1), jnp.stack(vs, axis=1), jnp.stack(ss, axis=1))
```
